```python
import math
import jax, jax.numpy as jnp
from jax import lax
import numpy as np

D_MODEL = 2048
BATCH = 1
SEQ = 16384
DEPTH = 1
DEC_BATCH = 32
DEC_SEQ = 4
PAST_LEN = 16384
PAGE_SIZE = 128

DK_A = 128
DV_A = 128
H_A = D_MODEL // 256
W_AK = H_A * DK_A
W_AV = H_A * DV_A
HGRN_CHUNK = 64
PATTERNS = ((128, 1), (512, 4), (2048, 16))
N_GROUPS = 3
HPG = D_MODEL // 512
H_B = N_GROUPS * HPG
HEAD_DIM = 128
W_B = H_B * HEAD_DIM
W_B_OUT = HPG * HEAD_DIM
D_FF = 256 * ((8 * D_MODEL // 3 + 255) // 256)
CONV_W = 3
IN_SIZES = (W_AK, W_AK, W_AV, W_AV, W_B, W_B, W_B, D_MODEL, D_MODEL)
N_IN = sum(IN_SIZES)
EPS = 1e-6

kernel_name = 'hgrn2_dilated_swa_hybrid_step'


def rmsnorm(x, w):
    xf = x.astype(jnp.float32)
    xf = xf * lax.rsqrt(jnp.mean(xf * xf, axis=-1, keepdims=True) + EPS)
    return (xf * w.astype(jnp.float32)).astype(x.dtype)


def alibi_slopes():
    s = 2.0 ** (-8.0 * np.arange(1, H_B + 1, dtype=np.float32) / H_B)
    return jnp.asarray(s, jnp.float32).reshape(N_GROUPS, HPG)


def split_in(z):
    offs = [int(o) for o in np.cumsum(IN_SIZES)[:-1]]
    return jnp.split(z, offs, axis=-1)


def hgrn2_scan(q, k, log_f, v, S0):
    B, T, H, DK = q.shape
    DV = v.shape[-1]
    C = math.gcd(T, HGRN_CHUNK)
    N = T // C

    def chunks(a):
        return a.astype(jnp.float32).reshape(B, N, C, H, a.shape[-1]).transpose(1, 0, 3, 2, 4)

    qc, kc, gc, vc = chunks(q), chunks(k), chunks(log_f), chunks(v)
    causal = jnp.tril(jnp.ones((C, C), dtype=bool))

    def step(S, inp):
        qi, ki, gi, vi = inp
        cum = jnp.cumsum(gi, axis=2)
        inter = jnp.einsum('bhtk,bhkv->bhtv', qi * jnp.exp(cum), S)
        rel = cum[:, :, :, None, :] - cum[:, :, None, :, :]
        decay = jnp.exp(jnp.where(causal[:, :, None], rel, -jnp.inf))
        scores = jnp.einsum('bhtk,bhsk,bhtsk->bhts', qi, ki, decay)
        intra = jnp.einsum('bhts,bhsv->bhtv', scores, vi)
        total = cum[:, :, -1:, :]
        kdec = ki * jnp.exp(total - cum)
        S_new = jnp.exp(total[:, :, 0, :])[..., None] * S + jnp.einsum('bhsk,bhsv->bhkv', kdec, vi)
        return S_new, inter + intra

    S_fin, o = lax.scan(step, S0.astype(jnp.float32), (qc, kc, gc, vc))
    o = o.transpose(1, 0, 3, 2, 4).reshape(B, T, H, DV)
    return o, S_fin


def softmax_stats(s):
    m = jnp.max(s, axis=-1, keepdims=True)
    p = jnp.exp(s - m)
    den = jnp.sum(p, axis=-1, keepdims=True)
    return p / den, (m + jnp.log(den))[..., 0]


def dilated_attn_prompt(q, k, v, slopes, window, dil):
    B, T, H, Dh = q.shape
    nb = window // dil
    blk = nb
    unit = dil * blk
    Tp = -(-T // unit) * unit
    M = Tp // dil
    NB = M // blk

    def res_blocks(a):
        a = jnp.pad(a, ((0, 0), (0, Tp - T), (0, 0), (0, 0)))
        a = a.reshape(B, M, dil, H, Dh).transpose(0, 2, 3, 1, 4)
        return a.reshape(B, dil, H, NB, blk, Dh)

    def with_prev(a):
        prev = jnp.concatenate([jnp.zeros_like(a[:, :, :, :1]), a[:, :, :, :-1]], axis=3)
        return jnp.concatenate([prev, a], axis=4)

    qb = res_blocks(q)
    kk = with_prev(res_blocks(k))
    vv = with_prev(res_blocks(v))
    s = jnp.einsum('brhnqd,brhnkd->brhnqk', qb, kk).astype(jnp.float32) * (Dh ** -0.5)
    i = jnp.arange(blk)[:, None]
    j = jnp.arange(2 * blk)[None, :]
    delta = blk + i - j
    key_idx = jnp.arange(NB)[:, None, None] * blk + j[None] - blk
    valid = ((delta >= 0) & (delta <= nb))[None] & (key_idx >= 0)
    bias = -slopes[:, None, None] * (delta * dil).astype(jnp.float32)[None]
    s = jnp.where(valid, s + bias[:, None], -jnp.inf)
    p, lse = softmax_stats(s)
    o = jnp.einsum('brhnqk,brhnkd->brhnqd', p, vv.astype(jnp.float32))
    o = o.reshape(B, dil, H, M, Dh).transpose(0, 3, 1, 2, 4).reshape(B, Tp, H, Dh)[:, :T]
    lse = lse.reshape(B, dil, H, M).transpose(0, 3, 1, 2).reshape(B, Tp, H)[:, :T]
    return o, lse


def dilated_attn_sample(q, k_new, v_new, kv_buf, slopes, window, dil):
    B, S, H, Dh = q.shape
    L = kv_buf.shape[1]
    nk = window // dil + 1
    k_all = jnp.concatenate([kv_buf[:, :, 0], k_new], axis=1)
    v_all = jnp.concatenate([kv_buf[:, :, 1], v_new], axis=1)
    i = jnp.arange(S)[:, None]
    j = jnp.arange(nk)[None, :]
    idx = L + i - j * dil
    valid = idx >= 0
    idx_c = jnp.maximum(idx, 0)
    kg = k_all[:, idx_c]
    vg = v_all[:, idx_c]
    s = jnp.einsum('bshd,bsjhd->bshj', q, kg).astype(jnp.float32) * (Dh ** -0.5)
    bias = -slopes[:, None] * (j[0] * dil).astype(jnp.float32)[None]
    s = jnp.where(valid[None, :, None, :], s + bias, -jnp.inf)
    p, lse = softmax_stats(s)
    o = jnp.einsum('bshj,bsjhd->bshd', p, vg.astype(jnp.float32))
    return o, lse


def conv_ffn(h, buf, w_up, conv_w, conv_b, w_down):
    T = h.shape[1]
    u = h @ w_up
    up = jnp.concatenate([buf.astype(u.dtype), u], axis=1)
    c = conv_b
    for j in range(CONV_W):
        c = c + conv_w[j] * up[:, j:j + T]
    g, a = jnp.split(c, 2, axis=-1)
    out = (jax.nn.silu(g) * a) @ w_down
    return out, up[:, -(CONV_W - 1):]


def trunk_layer(x, S0, kv_bufs, conv_buf, norm_mix_w, w_in, lb, hgrn_norm_w, w_proj_a, w_proj_b,
                w_out, norm_ffn_w, w_up, conv_w, conv_b, w_down, slopes):
    B, T, _ = x.shape
    h = rmsnorm(x, norm_mix_w)
    zq, zf, zi, zg, aq, ak, av, gate_a, gate_b = split_in(h @ w_in)
    f = lb + (1.0 - lb) * jax.nn.sigmoid(zf.astype(jnp.float32))
    q_a = jax.nn.silu(zq).reshape(B, T, H_A, DK_A)
    k_a = (1.0 - f).reshape(B, T, H_A, DK_A)
    logf_a = jnp.log(f).reshape(B, T, H_A, DK_A)
    v_a = zi.reshape(B, T, H_A, DV_A)
    o_a, S_new = hgrn2_scan(q_a, k_a, logf_a, v_a, S0)
    o_a = rmsnorm(o_a, hgrn_norm_w.reshape(H_A, DV_A)) * jax.nn.silu(zg.reshape(B, T, H_A, DV_A).astype(jnp.float32))
    y_a = o_a.astype(x.dtype).reshape(B, T, W_AV) @ w_proj_a
    q_b = aq.reshape(B, T, N_GROUPS, HPG, HEAD_DIM)
    k_b = ak.reshape(B, T, N_GROUPS, HPG, HEAD_DIM)
    v_b = av.reshape(B, T, N_GROUPS, HPG, HEAD_DIM)
    outs, lses, new_kv = [], [], []
    for g, (win, dil) in enumerate(PATTERNS):
        qg, kg, vg = q_b[:, :, g], k_b[:, :, g], v_b[:, :, g]
        kv_new = jnp.stack([kg, vg], axis=2)
        if kv_bufs is None:
            o, lse = dilated_attn_prompt(qg, kg, vg, slopes[g], win, dil)
            kv_state = kv_new[:, -min(win, T):]
        else:
            o, lse = dilated_attn_sample(qg, kg, vg, kv_bufs[g], slopes[g], win, dil)
            L = kv_bufs[g].shape[1]
            kv_state = jnp.concatenate([kv_bufs[g].astype(kv_new.dtype), kv_new], axis=1)[:, -L:]
        outs.append(o)
        lses.append(lse)
        new_kv.append(kv_state)
    wts = jax.nn.softmax(jnp.stack(lses, axis=0), axis=0)
    o_b = jnp.sum(wts[..., None] * jnp.stack(outs, axis=0), axis=0).astype(x.dtype)
    y_b = o_b.reshape(B, T, W_B_OUT) @ w_proj_b
    mix = jax.nn.sigmoid(gate_a) * y_a + jax.nn.sigmoid(gate_b) * y_b
    x = x + mix @ w_out
    ffn, conv_state = conv_ffn(rmsnorm(x, norm_ffn_w), conv_buf, w_up, conv_w, conv_b, w_down)
    x = x + ffn
    return x, S_new.astype(x.dtype), new_kv, conv_state


def setup_inputs(seed: int = 0) -> dict:
    key = jax.random.key(seed)
    ks = jax.random.split(key, 20)

    def nrm(k, shape, scale=1.0):
        return jax.random.normal(k, shape, jnp.float32) * scale

    L = [min(w, PAST_LEN) for w, _ in PATTERNS]
    return {
        'x_prompt': nrm(ks[0], (BATCH, SEQ, D_MODEL)),
        'x_sample': nrm(ks[1], (DEC_BATCH, DEC_SEQ, D_MODEL)),
        'state_hgrn': nrm(ks[2], (DEPTH, DEC_BATCH, H_A, DK_A, DV_A), 0.5),
        'cache_kv_g0': nrm(ks[3], (DEPTH, DEC_BATCH, L[0], 2, HPG, HEAD_DIM)),
        'cache_kv_g1': nrm(ks[4], (DEPTH, DEC_BATCH, L[1], 2, HPG, HEAD_DIM)),
        'cache_kv_g2': nrm(ks[5], (DEPTH, DEC_BATCH, L[2], 2, HPG, HEAD_DIM)),
        'state_ffn_conv': nrm(ks[6], (DEPTH, DEC_BATCH, CONV_W - 1, 2 * D_FF)),
        'norm_mix_w': 1.0 + nrm(ks[7], (DEPTH, D_MODEL), 0.02),
        'w_in': nrm(ks[8], (DEPTH, D_MODEL, N_IN), D_MODEL ** -0.5),
        'lb_logits': nrm(ks[9], (DEPTH + 1, W_AK), 0.5),
        'hgrn_norm_w': 1.0 + nrm(ks[10], (DEPTH, W_AV), 0.02),
        'w_proj_a': nrm(ks[11], (DEPTH, W_AV, D_MODEL), W_AV ** -0.5),
        'w_proj_b': nrm(ks[12], (DEPTH, W_B_OUT, D_MODEL), W_B_OUT ** -0.5),
        'w_out': nrm(ks[13], (DEPTH, D_MODEL, D_MODEL), D_MODEL ** -0.5),
        'norm_ffn_w': 1.0 + nrm(ks[14], (DEPTH, D_MODEL), 0.02),
        'w_up': nrm(ks[15], (DEPTH, D_MODEL, 2 * D_FF), D_MODEL ** -0.5),
        'ffn_conv_w': nrm(ks[16], (DEPTH, CONV_W, 2 * D_FF), CONV_W ** -0.5),
        'ffn_conv_b': nrm(ks[17], (DEPTH, 2 * D_FF), 0.02),
        'w_down': nrm(ks[18], (DEPTH, D_FF, D_MODEL), D_FF ** -0.5),
        'norm_final_w': 1.0 + nrm(ks[19], (D_MODEL,), 0.02),
    }


def reference(x_prompt, x_sample, state_hgrn, cache_kv_g0, cache_kv_g1, cache_kv_g2, state_ffn_conv,
              norm_mix_w, w_in, lb_logits, hgrn_norm_w, w_proj_a, w_proj_b, w_out, norm_ffn_w,
              w_up, ffn_conv_w, ffn_conv_b, w_down, norm_final_w):
    slopes = alibi_slopes()
    lb_all = jnp.cumsum(jax.nn.softmax(lb_logits.astype(jnp.float32), axis=0), axis=0)
    xp, xs = x_prompt, x_sample
    Bp = xp.shape[0]
    hg_p, hg_s, kv0_p, kv0_s, kv1_p, kv1_s, kv2_p, kv2_s, cv_p, cv_s = ([] for _ in range(10))
    for l in range(DEPTH):
        params = (norm_mix_w[l], w_in[l], lb_all[l], hgrn_norm_w[l], w_proj_a[l], w_proj_b[l], w_out[l],
                  norm_ffn_w[l], w_up[l], ffn_conv_w[l], ffn_conv_b[l], w_down[l], slopes)
        S0p = jnp.zeros((Bp, H_A, DK_A, DV_A), jnp.float32)
        conv0p = jnp.zeros((Bp, CONV_W - 1, 2 * D_FF), xp.dtype)
        xp, Sp, kvp, cp = trunk_layer(xp, S0p, None, conv0p, *params)
        xs, Ss, kvs, cs = trunk_layer(xs, state_hgrn[l], (cache_kv_g0[l], cache_kv_g1[l], cache_kv_g2[l]),
                                      state_ffn_conv[l], *params)
        hg_p.append(Sp); hg_s.append(Ss)
        kv0_p.append(kvp[0]); kv0_s.append(kvs[0])
        kv1_p.append(kvp[1]); kv1_s.append(kvs[1])
        kv2_p.append(kvp[2]); kv2_s.append(kvs[2])
        cv_p.append(cp); cv_s.append(cs)
    y_prompt = rmsnorm(xp, norm_final_w)
    y_sample = rmsnorm(xs, norm_final_w)
    return (y_prompt, y_sample, jnp.stack(hg_p), jnp.stack(hg_s), jnp.stack(kv0_p), jnp.stack(kv0_s),
            jnp.stack(kv1_p), jnp.stack(kv1_s), jnp.stack(kv2_p), jnp.stack(kv2_s),
            jnp.stack(cv_p), jnp.stack(cv_s))
```

```python
import functools

import numpy as np
import jax
import jax.numpy as jnp
from jax import lax
from jax.experimental import pallas as pl
from jax.experimental.pallas import tpu as pltpu

F32 = jnp.float32
BF16 = jnp.bfloat16

LANES = 128
EPS = 1e-6
NEG = -1e30
VMEM_LIMIT = 56 * 1024 * 1024

N_HA = 8
N_GROUPS = 3
HPG = 4
PATTERNS = ((128, 1), (512, 4), (2048, 16))
NKEYS = 128
SLOT_A = 0
SLOT_QKV = 32
SLOT_GATE = 68
N_SLOTS = 100


def _dot(a, b):
    return jnp.dot(a, b, preferred_element_type=F32)


def _dot_nt(a, b):
    return lax.dot_general(a, b, (((1,), (1,)), ((), ())), preferred_element_type=F32)


def _dot_tn(a, b):
    return lax.dot_general(a, b, (((0,), (0,)), ((), ())), preferred_element_type=F32)


def _sigmoid(x):
    return 1.0 / (1.0 + jnp.exp(-x))


def _params(*sem):
    return pltpu.CompilerParams(dimension_semantics=sem, vmem_limit_bytes=VMEM_LIMIT)


def _inproj_kernel(x_ref, nw_ref, w_ref, o_ref, h_ref):
    @pl.when(pl.program_id(1) == 0)
    def _():
        x = x_ref[...]
        ms = jnp.mean(x * x, axis=-1, keepdims=True)
        h_ref[...] = (x * lax.rsqrt(ms + EPS) * nw_ref[...]).astype(BF16)

    z = _dot(h_ref[...], w_ref[...])
    for c in range(o_ref.shape[0]):
        o_ref[c] = z[:, c * LANES:(c + 1) * LANES].astype(o_ref.dtype)


def _inproj(x, nw, w, slot0, n_slots, out_dtype, tm, tn=512):
    m, d = x.shape
    spt = tn // LANES
    assert m % tm == 0 and n_slots % spt == 0 and slot0 % spt == 0
    return pl.pallas_call(
        _inproj_kernel,
        grid=(m // tm, n_slots // spt),
        in_specs=[
            pl.BlockSpec((tm, d), lambda i, j: (i, 0)),
            pl.BlockSpec((1, d), lambda i, j: (0, 0)),
            pl.BlockSpec((d, tn), lambda i, j: (0, j + slot0 // spt)),
        ],
        out_specs=pl.BlockSpec((spt, tm, LANES), lambda i, j: (j, i, 0)),
        out_shape=jax.ShapeDtypeStruct((n_slots, m, LANES), out_dtype),
        scratch_shapes=[pltpu.VMEM((tm, d), BF16)],
        compiler_params=_params("parallel", "arbitrary"),
        name="inproj",
    )(x, nw, w)


def _hgrn_levels(c):
    levels, h = [], c // 2
    while h >= 1:
        levels.append(h)
        h //= 2
    return levels


def _hgrn_matrix(c):
    r = np.arange(c)
    low = (r[None, :] <= r[:, None]).astype(np.float32)
    mats = [low, 1.0 - low]
    for h in _hgrn_levels(c):
        mid = (r // (2 * h)) * (2 * h) + h
        ref = (r[None, :] <= (mid - 1)[:, None]).astype(np.float32)
        mats.append(low - ref)
    return np.concatenate(mats, axis=0)


def _hgrn_kernel(zq_ref, zf_ref, zi_ref, zg_ref, s0_ref, lb_ref, nw_ref, a_ref, o_ref, sout_ref,
                 st_ref, *, chunk, valid, n_chunks):
    c = chunk
    levels = _hgrn_levels(c)
    n = pl.program_id(1)

    @pl.when(n == 0)
    def _():
        for h in range(N_HA):
            st_ref[h] = s0_ref[0, h].T

    row = lax.broadcasted_iota(jnp.int32, (c, 1), 0)
    ti = lax.broadcasted_iota(jnp.int32, (c, c), 0)
    si = lax.broadcasted_iota(jnp.int32, (c, c), 1)
    eye = ti == si
    same = [(ti // (2 * h)) == (si // (2 * h)) for h in levels]
    upper = [(row % (2 * h)) >= h for h in levels]
    a_mat = a_ref[...]

    def one_chunk(ci, carry):
        r0 = pl.multiple_of(ci * c, c)
        for h in range(N_HA):
            zq = zq_ref[h, pl.ds(r0, c), :].astype(F32)
            zf = zf_ref[h, pl.ds(r0, c), :].astype(F32)
            v = zi_ref[h, pl.ds(r0, c), :].astype(F32)
            zg = zg_ref[h, pl.ds(r0, c), :].astype(F32)
            lb = lb_ref[h]
            f = lb + (1.0 - lb) * _sigmoid(zf)
            logf = jnp.log(f)
            k = 1.0 - f
            if valid < c:
                live = row < valid
                logf = jnp.where(live, logf, 0.0)
                k = jnp.where(live, k, 0.0)
            q = zq * _sigmoid(zq)
            hi = logf.astype(BF16)
            r1 = logf - hi.astype(F32)
            mid = r1.astype(BF16)
            lo = (r1 - mid.astype(F32)).astype(BF16)
            rs = _dot(a_mat, jnp.concatenate([hi, mid, lo], axis=1))
            rs = rs[:, :LANES] + rs[:, LANES:2 * LANES] + rs[:, 2 * LANES:]
            cum = rs[0:c]
            rev = rs[c:2 * c]
            total = cum[c - 1:c]
            st = st_ref[h]
            vb = v.astype(BF16)
            inter = _dot_nt((q * jnp.exp(cum)).astype(BF16), st.astype(BF16))
            scores = jnp.where(eye, _dot_nt(q.astype(BF16), k.astype(BF16)), 0.0)
            for li in range(len(levels)):
                e = jnp.exp(-jnp.abs(rs[(2 + li) * c:(3 + li) * c]))
                ql = jnp.where(upper[li], q * e, 0.0).astype(BF16)
                kl = jnp.where(upper[li], 0.0, k * e).astype(BF16)
                scores = scores + jnp.where(same[li], _dot_nt(ql, kl), 0.0)
            o = inter + _dot(scores.astype(BF16), vb)
            kdec = (k * jnp.exp(rev)).astype(BF16)
            st_ref[h] = st * jnp.exp(total) + _dot_tn(vb, kdec)
            ms = jnp.mean(o * o, axis=-1, keepdims=True)
            on = o * lax.rsqrt(ms + EPS) * nw_ref[h] * (zg * _sigmoid(zg))
            o_ref[h, pl.ds(r0, c), :] = on.astype(o_ref.dtype)
        return carry

    lax.fori_loop(0, n_chunks, one_chunk, 0)

    @pl.when(n == pl.num_programs(1) - 1)
    def _():
        for h in range(N_HA):
            sout_ref[0, h] = st_ref[h].T


def _hgrn(z_a, s0, lb, nw, *, n_seq, rows_per_seq, chunk, valid, rows_per_step):
    m = z_a.shape[1]
    assert m == n_seq * rows_per_seq and rows_per_seq % rows_per_step == 0 and rows_per_step % chunk == 0
    nb = rows_per_seq // rows_per_step
    a_mat = jnp.asarray(_hgrn_matrix(chunk), BF16)
    zspec = lambda s: pl.BlockSpec((N_HA, rows_per_step, LANES), lambda b, n: (s, b * nb + n, 0))
    sspec = pl.BlockSpec((1, N_HA, LANES, LANES), lambda b, n: (b, 0, 0, 0))
    vspec = pl.BlockSpec((N_HA, 1, LANES), lambda b, n: (0, 0, 0))
    return pl.pallas_call(
        functools.partial(_hgrn_kernel, chunk=chunk, valid=valid, n_chunks=rows_per_step // chunk),
        grid=(n_seq, nb),
        in_specs=[zspec(0), zspec(1), zspec(2), zspec(3), sspec, vspec, vspec,
                  pl.BlockSpec(a_mat.shape, lambda b, n: (0, 0))],
        out_specs=[pl.BlockSpec((N_HA, rows_per_step, LANES), lambda b, n: (0, b * nb + n, 0)), sspec],
        out_shape=[jax.ShapeDtypeStruct((N_HA, m, LANES), BF16),
                   jax.ShapeDtypeStruct((n_seq, N_HA, LANES, LANES), F32)],
        scratch_shapes=[pltpu.VMEM((N_HA, LANES, LANES), F32)],
        compiler_params=_params("parallel", "arbitrary"),
        name="hgrn",
    )(z_a, z_a, z_a, z_a, s0, lb, nw, a_mat)


ATT_TILE = 2048


def _band_softmax(q, kp, ko, vp, vo, prev_limit, slope_d, dist_p, dist_o, scale):
    qb = q.astype(BF16)
    sp = _dot_nt(qb, kp.astype(BF16)) * scale - slope_d * dist_p
    so = _dot_nt(qb, ko.astype(BF16)) * scale - slope_d * dist_o
    sp = jnp.where(dist_p <= prev_limit, sp, NEG)
    so = jnp.where(dist_o >= 0.0, so, NEG)
    m = jnp.maximum(jnp.max(sp, axis=-1, keepdims=True), jnp.max(so, axis=-1, keepdims=True))
    pp = jnp.exp(sp - m)
    po = jnp.exp(so - m)
    den = jnp.sum(pp, axis=-1, keepdims=True) + jnp.sum(po, axis=-1, keepdims=True)
    acc = _dot(pp.astype(BF16), vp.astype(BF16)) + _dot(po.astype(BF16), vo.astype(BF16))
    return acc / den, m + jnp.log(den)


def _attn_prompt_kernel(slopes_ref, *refs):
    ins, o_ref, scr = refs[:15], refs[15], refs[16:]
    h = pl.program_id(0)
    n = pl.program_id(1)
    blk = NKEYS
    scale = float(LANES) ** -0.5
    ti = lax.broadcasted_iota(jnp.int32, (blk, blk), 0)
    si = lax.broadcasted_iota(jnp.int32, (blk, blk), 1)
    dist_o = (ti - si).astype(F32)
    dist_p = dist_o + float(blk)
    for g, (_, dil) in enumerate(PATTERNS):
        q_ref, ko_ref, kp_ref, vo_ref, vp_ref = ins[5 * g:5 * g + 5]
        og_ref, lg_ref = scr[2 * g], scr[2 * g + 1]
        slope_d = slopes_ref[g, h] * float(dil)
        span = blk * dil
        nj = ATT_TILE // span

        def unit(start, prev_start, from_prev_tile):
            rows = pl.ds(start, blk, stride=dil) if dil > 1 else pl.ds(start, blk)
            prows = pl.ds(prev_start, blk, stride=dil) if dil > 1 else pl.ds(prev_start, blk)
            kp = kp_ref[0, prows, :] if from_prev_tile else ko_ref[0, prows, :]
            vp = vp_ref[0, prows, :] if from_prev_tile else vo_ref[0, prows, :]
            prev_limit = jnp.where(n > 0, float(blk), -1.0) if from_prev_tile else float(blk)
            o, lse = _band_softmax(q_ref[0, rows, :], kp, ko_ref[0, rows, :], vp, vo_ref[0, rows, :],
                                   prev_limit, slope_d, dist_p, dist_o, scale)
            og_ref[rows, :] = o
            lg_ref[rows, :] = jnp.broadcast_to(lse, (blk, LANES))

        def first_block(r, carry):
            unit(r, r, True)
            return carry

        lax.fori_loop(0, dil, first_block, 0)
        if nj > 1:
            def later_block(u, carry):
                j = u // dil + 1
                r = u % dil
                unit(j * span + r, (j - 1) * span + r, False)
                return carry

            lax.fori_loop(0, (nj - 1) * dil, later_block, 0)

    rows_per_pass = 256

    def merge(c, carry):
        rows = pl.ds(pl.multiple_of(c * rows_per_pass, rows_per_pass), rows_per_pass)
        l0, l1, l2 = scr[1][rows, :], scr[3][rows, :], scr[5][rows, :]
        mx = jnp.maximum(jnp.maximum(l0, l1), l2)
        w0, w1, w2 = jnp.exp(l0 - mx), jnp.exp(l1 - mx), jnp.exp(l2 - mx)
        num = w0 * scr[0][rows, :] + w1 * scr[2][rows, :] + w2 * scr[4][rows, :]
        o_ref[rows, :] = (num / (w0 + w1 + w2)).astype(o_ref.dtype)
        return carry

    lax.fori_loop(0, ATT_TILE // rows_per_pass, merge, 0)


def _attn_prompt(z_qkv, slopes):
    t = z_qkv.shape[1]
    assert t % ATT_TILE == 0
    in_specs = [pl.BlockSpec(memory_space=pltpu.SMEM)]
    args = [slopes]
    for g, (win, dil) in enumerate(PATTERNS):
        assert win // dil == NKEYS
        prev_rows = NKEYS * dil
        ratio = ATT_TILE // prev_rows
        own = lambda base, g=g: pl.BlockSpec((1, ATT_TILE, LANES), lambda h, n: (base + g * HPG + h, n, 0))
        prev = lambda base, g=g, ratio=ratio, prev_rows=prev_rows: pl.BlockSpec(
            (1, prev_rows, LANES), lambda h, n: (base + g * HPG + h, jnp.maximum(n * ratio - 1, 0), 0))
        in_specs += [own(0), own(12), prev(12), own(24), prev(24)]
        args += [z_qkv] * 5
    return pl.pallas_call(
        _attn_prompt_kernel,
        grid=(HPG, t // ATT_TILE),
        in_specs=in_specs,
        out_specs=pl.BlockSpec((ATT_TILE, LANES), lambda h, n: (n, h)),
        out_shape=jax.ShapeDtypeStruct((t, HPG * LANES), BF16),
        scratch_shapes=[pltpu.VMEM((ATT_TILE, LANES), F32)] * 6,
        compiler_params=_params("parallel", "arbitrary"),
        name="attn_prompt",
    )(*args)


def _attn_sample_kernel(slopes_ref, z_ref, c0_ref, c1_ref, c2_ref, o_ref, *scr, n_new):
    caches = (c0_ref, c1_ref, c2_ref)
    bufs, sem = scr[:3], scr[3]
    b = pl.program_id(0)
    s = n_new
    scale = float(LANES) ** -0.5

    def copies():
        out = []
        for g in range(N_GROUPS):
            for kv in range(2):
                for h in range(HPG):
                    out.append(pltpu.make_async_copy(caches[g].at[0, b, :, kv, h, :], bufs[g].at[kv, h],
                                                     sem.at[g, kv, h]))
        return out

    for cp in copies():
        cp.start()
    for cp in copies():
        cp.wait()

    for h in range(HPG):
        outs, lses = [], []
        for g, (win, dil) in enumerate(PATTERNS):
            length = bufs[g].shape[2]
            q = z_ref[g * HPG + h, 0]
            kn = z_ref[12 + g * HPG + h, 0]
            vn = z_ref[24 + g * HPG + h, 0]
            slope = slopes_ref[g, h]
            qb = q.astype(BF16)
            qi = lax.broadcasted_iota(jnp.int32, (s, length), 0)
            ri = lax.broadcasted_iota(jnp.int32, (s, length), 1)
            dist = length + qi - ri
            ok = (dist <= win) & ((dist & (dil - 1)) == 0)
            sc = _dot_nt(qb, bufs[g][0, h].astype(BF16)) * scale - slope * dist.astype(F32)
            sc = jnp.where(ok, sc, NEG)
            qn = lax.broadcasted_iota(jnp.int32, (s, s), 0)
            rn = lax.broadcasted_iota(jnp.int32, (s, s), 1)
            dn = qn - rn
            okn = (dn >= 0) & ((dn & (dil - 1)) == 0)
            sn = _dot_nt(qb, kn.astype(BF16)) * scale - slope * dn.astype(F32)
            sn = jnp.where(okn, sn, NEG)
            m = jnp.maximum(jnp.max(sc, axis=-1, keepdims=True), jnp.max(sn, axis=-1, keepdims=True))
            pc = jnp.exp(sc - m)
            pn = jnp.exp(sn - m)
            den = jnp.sum(pc, axis=-1, keepdims=True) + jnp.sum(pn, axis=-1, keepdims=True)
            acc = _dot(pc.astype(BF16), bufs[g][1, h].astype(BF16)) + _dot(pn.astype(BF16), vn.astype(BF16))
            outs.append(acc / den)
            lses.append(m + jnp.log(den))
        mx = jnp.maximum(jnp.maximum(lses[0], lses[1]), lses[2])
        ws = [jnp.exp(l - mx) for l in lses]
        num = ws[0] * outs[0] + ws[1] * outs[1] + ws[2] * outs[2]
        o_ref[0, :, h * LANES:(h + 1) * LANES] = num / (ws[0] + ws[1] + ws[2])


def _attn_sample(z_qkv, caches, slopes, n_seq, n_new):
    z4 = z_qkv.reshape(3 * N_GROUPS * HPG, n_seq, n_new, LANES)
    for g, (win, _) in enumerate(PATTERNS):
        assert caches[g].shape[2] == win, "window buffers shorter than the window are not supported"
    bufs = [pltpu.VMEM((2, HPG, c.shape[2], LANES), F32) for c in caches]
    return pl.pallas_call(
        functools.partial(_attn_sample_kernel, n_new=n_new),
        grid=(n_seq,),
        in_specs=[pl.BlockSpec(memory_space=pltpu.SMEM),
                  pl.BlockSpec((z4.shape[0], 1, n_new, LANES), lambda b: (0, b, 0, 0)),
                  pl.BlockSpec(memory_space=pl.ANY), pl.BlockSpec(memory_space=pl.ANY),
                  pl.BlockSpec(memory_space=pl.ANY)],
        out_specs=pl.BlockSpec((1, n_new, HPG * LANES), lambda b: (b, 0, 0)),
        out_shape=jax.ShapeDtypeStruct((n_seq, n_new, HPG * LANES), F32),
        scratch_shapes=bufs + [pltpu.SemaphoreType.DMA((N_GROUPS, 2, HPG))],
        compiler_params=_params("arbitrary"),
        name="attn_sample",
    )(slopes, z4, *caches)


def _mix_kernel(oa_ref, ob_ref, ga_ref, gb_ref, x_ref, wa_ref, wb_ref, wo_ref, o_ref):
    oa = jnp.concatenate([oa_ref[h] for h in range(oa_ref.shape[0])], axis=1)
    ya = _dot(oa, wa_ref[...])
    yb = _dot(ob_ref[...], wb_ref[...])
    parts = []
    for c in range(ga_ref.shape[0]):
        cols = slice(c * LANES, (c + 1) * LANES)
        mix = _sigmoid(ga_ref[c].astype(F32)) * ya[:, cols] + _sigmoid(gb_ref[c].astype(F32)) * yb[:, cols]
        parts.append(mix.astype(BF16))
    o_ref[...] = x_ref[...] + _dot(jnp.concatenate(parts, axis=1), wo_ref[...])


def _mix(o_a, o_b, z_gate, x, w_a, w_b, w_o, tm):
    m, d = x.shape
    ng = d // LANES
    assert m % tm == 0
    full = lambda a: pl.BlockSpec(a.shape, lambda i: (0, 0))
    return pl.pallas_call(
        _mix_kernel,
        grid=(m // tm,),
        in_specs=[pl.BlockSpec((N_HA, tm, LANES), lambda i: (0, i, 0)),
                  pl.BlockSpec((tm, o_b.shape[1]), lambda i: (i, 0)),
                  pl.BlockSpec((ng, tm, LANES), lambda i: (0, i, 0)),
                  pl.BlockSpec((ng, tm, LANES), lambda i: (1, i, 0)),
                  pl.BlockSpec((tm, d), lambda i: (i, 0)),
                  full(w_a), full(w_b), full(w_o)],
        out_specs=pl.BlockSpec((tm, d), lambda i: (i, 0)),
        out_shape=jax.ShapeDtypeStruct((m, d), F32),
        compiler_params=_params("parallel"),
        name="mix",
    )(o_a, o_b, z_gate, z_gate, x, w_a, w_b, w_o)


def _ffn_kernel(*refs, tm, tf, seq_rows):
    (x_ref, nw_ref, wug_ref, wua_ref, cwg_ref, cwa_ref, cbg_ref, cba_ref, wd_ref, fw_ref) = refs[:10]
    if seq_rows is None:
        y_ref, sg_ref, sa_ref, h_ref, acc_ref, ug_ref, ua_ref, carry_ref = refs[10:]
    else:
        pg_ref, pa_ref, y_ref, sg_ref, sa_ref, h_ref, acc_ref = refs[10:]
    i = pl.program_id(0)
    j = pl.program_id(1)

    @pl.when(j == 0)
    def _():
        x = x_ref[...]
        ms = jnp.mean(x * x, axis=-1, keepdims=True)
        h_ref[...] = (x * lax.rsqrt(ms + EPS) * nw_ref[...]).astype(BF16)
        acc_ref[...] = jnp.zeros_like(acc_ref)

    if seq_rows is None:
        @pl.when((i == 0) & (j == 0))
        def _():
            carry_ref[...] = jnp.zeros_like(carry_ref)

    def conv(part, wu_ref, cw_ref, cb_ref):
        u = _dot(h_ref[...], wu_ref[...])
        if seq_rows is None:
            ub_ref, s_ref = ((ug_ref, sg_ref), (ua_ref, sa_ref))[part]
            cols = pl.ds(pl.multiple_of(j * tf, tf), tf)
            ub_ref[pl.ds(0, 8), :] = carry_ref[part, :, cols]
            ub_ref[pl.ds(8, tm), :] = u
            u1 = ub_ref[pl.ds(7, tm), :]
            u2 = ub_ref[pl.ds(6, tm), :]
            carry_ref[part, :, cols] = ub_ref[pl.ds(tm, 8), :]
            s_ref[0] = ub_ref[pl.ds(tm + 6, 2), :]
        else:
            p_ref, s_ref = ((pg_ref, sg_ref), (pa_ref, sa_ref))[part]
            pos = lax.broadcasted_iota(jnp.int32, (tm, 1), 0) % seq_rows
            u1 = jnp.where(pos == 0, p_ref[1], pltpu.roll(u, 1, 0))
            u2 = jnp.where(pos == 0, p_ref[0], jnp.where(pos == 1, p_ref[1], pltpu.roll(u, 2, 0)))
            s_ref[...] = u
        return cb_ref[...] + cw_ref[2:3, :] * u + cw_ref[1:2, :] * u1 + cw_ref[0:1, :] * u2

    cg = conv(0, wug_ref, cwg_ref, cbg_ref)
    ca = conv(1, wua_ref, cwa_ref, cba_ref)
    act = (cg * _sigmoid(cg) * ca).astype(BF16)
    acc_ref[...] += _dot(act, wd_ref[...])

    @pl.when(j == pl.num_programs(1) - 1)
    def _():
        x2 = x_ref[...] + acc_ref[...]
        ms = jnp.mean(x2 * x2, axis=-1, keepdims=True)
        y_ref[...] = x2 * lax.rsqrt(ms + EPS) * fw_ref[...]


def _ffn(x1, nw, w_up, conv_w, conv_b, w_down, fw, *, tm, tf, seq_rows=None, prev=None):
    m, d = x1.shape
    dff = w_down.shape[0]
    assert m % tm == 0 and dff % tf == 0
    nf = dff // tf
    row = lambda i, j: (i, 0)
    const = lambda i, j: (0, 0)
    gcol = lambda i, j: (0, j)
    acol = lambda i, j: (0, nf + j)
    in_specs = [pl.BlockSpec((tm, d), row), pl.BlockSpec((1, d), const),
                pl.BlockSpec((d, tf), gcol), pl.BlockSpec((d, tf), acol),
                pl.BlockSpec((3, tf), gcol), pl.BlockSpec((3, tf), acol),
                pl.BlockSpec((1, tf), gcol), pl.BlockSpec((1, tf), acol),
                pl.BlockSpec((tf, d), lambda i, j: (j, 0)), pl.BlockSpec((1, d), const)]
    args = [x1, nw, w_up, w_up, conv_w, conv_w, conv_b, conv_b, w_down, fw]
    scratch = [pltpu.VMEM((tm, d), BF16), pltpu.VMEM((tm, d), F32)]
    if seq_rows is None:
        state = jax.ShapeDtypeStruct((m // tm, 2, dff), F32)
        state_spec = pl.BlockSpec((1, 2, tf), lambda i, j: (i, 0, j))
        scratch += [pltpu.VMEM((tm + 8, tf), F32), pltpu.VMEM((tm + 8, tf), F32),
                    pltpu.VMEM((2, 8, dff), F32)]
    else:
        assert m // tm == 1 and tm % seq_rows == 0
        in_specs += [pl.BlockSpec((2, tm, tf), lambda i, j: (0, 0, j)),
                     pl.BlockSpec((2, tm, tf), lambda i, j: (0, 0, nf + j))]
        args += [prev, prev]
        state = jax.ShapeDtypeStruct((m, dff), F32)
        state_spec = pl.BlockSpec((tm, tf), lambda i, j: (i, j))
    return pl.pallas_call(
        functools.partial(_ffn_kernel, tm=tm, tf=tf, seq_rows=seq_rows),
        grid=(m // tm, nf),
        in_specs=in_specs,
        out_specs=[pl.BlockSpec((tm, d), row), state_spec, state_spec],
        out_shape=[jax.ShapeDtypeStruct((m, d), F32), state, state],
        scratch_shapes=scratch,
        compiler_params=_params("arbitrary", "arbitrary"),
        name="ffn",
    )(*args)


def _alibi_slopes():
    n = N_GROUPS * HPG
    s = 2.0 ** (-8.0 * np.arange(1, n + 1, dtype=np.float32) / n)
    return jnp.asarray(s, F32).reshape(N_GROUPS, HPG)


def _kv_rows(z_qkv, g, rows):
    k = z_qkv[12 + g * HPG:12 + (g + 1) * HPG, rows]
    v = z_qkv[24 + g * HPG:24 + (g + 1) * HPG, rows]
    return jnp.stack([k, v], axis=0).transpose(2, 0, 1, 3)


def kernel(x_prompt, x_sample, state_hgrn, cache_kv_g0, cache_kv_g1, cache_kv_g2, state_ffn_conv,
           norm_mix_w, w_in, lb_logits, hgrn_norm_w, w_proj_a, w_proj_b, w_out, norm_ffn_w,
           w_up, ffn_conv_w, ffn_conv_b, w_down, norm_final_w):
    assert w_in.shape[0] == 1, "single-layer trunk"
    _, t, d = x_prompt.shape
    n_seq, n_new, _ = x_sample.shape
    dff = w_down.shape[1]
    slopes = _alibi_slopes()
    lb = jnp.cumsum(jax.nn.softmax(lb_logits.astype(F32), axis=0), axis=0)[0].reshape(N_HA, 1, LANES)
    hnw = hgrn_norm_w[0].reshape(N_HA, 1, LANES)
    nmw = norm_mix_w[0].reshape(1, d)
    nfw = norm_ffn_w[0].reshape(1, d)
    fw = norm_final_w.reshape(1, d)
    w_in_b = w_in[0].astype(BF16)
    w_a_b = w_proj_a[0].astype(BF16)
    w_b_b = w_proj_b[0].astype(BF16)
    w_o_b = w_out[0].astype(BF16)
    w_up_b = w_up[0].astype(BF16)
    w_dn_b = w_down[0].astype(BF16)
    cw = ffn_conv_w[0]
    cb = ffn_conv_b[0].reshape(1, 2 * dff)
    caches = (cache_kv_g0, cache_kv_g1, cache_kv_g2)

    def project(x, tm):
        z_a = _inproj(x, nmw, w_in_b, SLOT_A, 32, BF16, tm)
        z_qkv = _inproj(x, nmw, w_in_b, SLOT_QKV, 36, F32, tm)
        z_gate = _inproj(x, nmw, w_in_b, SLOT_GATE, 32, BF16, tm)
        return z_a, z_qkv, z_gate

    xp = x_prompt[0]
    z_a, z_qkv, z_gate = project(xp, 1024)
    o_a, s_p = _hgrn(z_a, jnp.zeros((1, N_HA, LANES, LANES), F32), lb, hnw,
                     n_seq=1, rows_per_seq=t, chunk=64, valid=64, rows_per_step=512)
    o_b = _attn_prompt(z_qkv, slopes)
    x1 = _mix(o_a, o_b, z_gate, xp, w_a_b, w_b_b, w_o_b, 256)
    y_p, sg, sa = _ffn(x1, nfw, w_up_b, cw, cb, w_dn_b, fw, tm=512, tf=512)
    conv_p = jnp.concatenate([sg[-1], sa[-1]], axis=1)[None, None]
    kv_p = [_kv_rows(z_qkv, g, slice(t - min(win, t), t))[None, None] for g, (win, _) in enumerate(PATTERNS)]

    ms = n_seq * n_new
    xs = x_sample.reshape(ms, d)
    z_a, z_qkv, z_gate = project(xs, ms)
    pad = 16
    z_a_pad = jnp.pad(z_a.reshape(32, n_seq, n_new, LANES), ((0, 0), (0, 0), (0, pad - n_new), (0, 0)))
    o_a, s_s = _hgrn(z_a_pad.reshape(32, n_seq * pad, LANES), state_hgrn[0], lb, hnw,
                     n_seq=n_seq, rows_per_seq=pad, chunk=pad, valid=n_new, rows_per_step=pad)
    o_a = o_a.reshape(N_HA, n_seq, pad, LANES)[:, :, :n_new].reshape(N_HA, ms, LANES)
    o_b = _attn_sample(z_qkv, caches, slopes, n_seq, n_new).reshape(ms, HPG * LANES).astype(BF16)
    x1 = _mix(o_a, o_b, z_gate, xs, w_a_b, w_b_b, w_o_b, ms)
    prev = jnp.repeat(state_ffn_conv[0].transpose(1, 0, 2), n_new, axis=1)
    y_s, ug, ua = _ffn(x1, nfw, w_up_b, cw, cb, w_dn_b, fw, tm=ms, tf=512, seq_rows=n_new, prev=prev)
    u = jnp.concatenate([ug, ua], axis=1).reshape(n_seq, n_new, 2 * dff)
    conv_s = u[:, n_new - 2:][None]
    kv_s = []
    for g in range(N_GROUPS):
        new = _kv_rows(z_qkv, g, slice(0, ms)).reshape(n_seq, n_new, 2, HPG, LANES)
        length = caches[g].shape[2]
        kv_s.append(jnp.concatenate([caches[g][0], new], axis=1)[:, -length:][None])

    return (y_p[None], y_s.reshape(n_seq, n_new, d), s_p[None], s_s[None],
            kv_p[0], kv_s[0], kv_p[1], kv_s[1], kv_p[2], kv_s[2], conv_p, conv_s)
```

```python
import functools

import numpy as np
import jax
import jax.numpy as jnp
from jax import lax
from jax.experimental import pallas as pl
from jax.experimental.pallas import tpu as pltpu

F32 = jnp.float32
BF16 = jnp.bfloat16

LANES = 128
EPS = 1e-6
NEG = -1e30
VMEM_LIMIT = 56 * 1024 * 1024

N_HA = 8
N_GROUPS = 3
HPG = 4
PATTERNS = ((128, 1), (512, 4), (2048, 16))
NKEYS = 128
SLOT_A = 0
SLOT_GATE = 32
SLOT_Q = 64
SLOT_K = 76
SLOT_V = 88


def _dot(a, b):
    return jnp.dot(a, b, preferred_element_type=F32)


def _dot_nt(a, b):
    return lax.dot_general(a, b, (((1,), (1,)), ((), ())), preferred_element_type=F32)


def _dot_tn(a, b):
    return lax.dot_general(a, b, (((0,), (0,)), ((), ())), preferred_element_type=F32)


def _sigmoid(x):
    return 1.0 / (1.0 + jnp.exp(-x))


def _params(*sem):
    return pltpu.CompilerParams(dimension_semantics=sem, vmem_limit_bytes=VMEM_LIMIT)


def _inproj_kernel(x_ref, nw_ref, w_ref, o_ref, h_ref):
    @pl.when(pl.program_id(1) == 0)
    def _():
        x = x_ref[...]
        ms = jnp.mean(x * x, axis=-1, keepdims=True)
        h_ref[...] = (x * lax.rsqrt(ms + EPS) * nw_ref[...]).astype(BF16)

    z = _dot(h_ref[...], w_ref[...])
    for c in range(o_ref.shape[0]):
        o_ref[c] = z[:, c * LANES:(c + 1) * LANES].astype(o_ref.dtype)


def _inproj(x, nw, w, tm, tn):
    m, d = x.shape
    n_slots = w.shape[1] // LANES
    spt = tn // LANES
    assert m % tm == 0 and n_slots % spt == 0
    return pl.pallas_call(
        _inproj_kernel,
        grid=(m // tm, n_slots // spt),
        in_specs=[
            pl.BlockSpec((tm, d), lambda i, j: (i, 0)),
            pl.BlockSpec((1, d), lambda i, j: (0, 0)),
            pl.BlockSpec((d, tn), lambda i, j: (0, j)),
        ],
        out_specs=pl.BlockSpec((spt, tm, LANES), lambda i, j: (j, i, 0)),
        out_shape=jax.ShapeDtypeStruct((n_slots, m, LANES), F32),
        scratch_shapes=[pltpu.VMEM((tm, d), BF16)],
        compiler_params=_params("parallel", "arbitrary"),
        name="inproj",
    )(x, nw, w)


def _hgrn_levels(c):
    levels, h = [], c // 2
    while h >= 1:
        levels.append(h)
        h //= 2
    return levels


def _hgrn_matrix(c):
    r = np.arange(c)
    low = (r[None, :] <= r[:, None]).astype(np.float32)
    mats = [low, 1.0 - low]
    for h in _hgrn_levels(c):
        mid = (r // (2 * h)) * (2 * h) + h
        ref = (r[None, :] <= (mid - 1)[:, None]).astype(np.float32)
        mats.append(low - ref)
    return np.concatenate(mats, axis=0)


def _hgrn_kernel(zq_ref, zf_ref, zi_ref, zg_ref, s0_ref, lb_ref, nw_ref, a_ref, o_ref, sout_ref,
                 st_ref, *, chunk, valid, n_chunks):
    c = chunk
    levels = _hgrn_levels(c)
    n = pl.program_id(1)

    @pl.when(n == 0)
    def _():
        for h in range(N_HA):
            st_ref[h] = s0_ref[0, h].T

    row = lax.broadcasted_iota(jnp.int32, (c, 1), 0)
    ti = lax.broadcasted_iota(jnp.int32, (c, c), 0)
    si = lax.broadcasted_iota(jnp.int32, (c, c), 1)
    eye = ti == si
    same = [(ti // (2 * h)) == (si // (2 * h)) for h in levels]
    upper = [(row % (2 * h)) >= h for h in levels]
    a_mat = a_ref[...]

    def one_chunk(ci, carry):
        r0 = pl.multiple_of(ci * c, c)
        for h in range(N_HA):
            zq = zq_ref[h, pl.ds(r0, c), :].astype(F32)
            zf = zf_ref[h, pl.ds(r0, c), :].astype(F32)
            v = zi_ref[h, pl.ds(r0, c), :].astype(F32)
            zg = zg_ref[h, pl.ds(r0, c), :].astype(F32)
            lb = lb_ref[h]
            f = lb + (1.0 - lb) * _sigmoid(zf)
            logf = jnp.log(f)
            k = 1.0 - f
            if valid < c:
                live = row < valid
                logf = jnp.where(live, logf, 0.0)
                k = jnp.where(live, k, 0.0)
            q = zq * _sigmoid(zq)
            hi = logf.astype(BF16)
            r1 = logf - hi.astype(F32)
            mid = r1.astype(BF16)
            lo = (r1 - mid.astype(F32)).astype(BF16)
            rs = _dot(a_mat, jnp.concatenate([hi, mid, lo], axis=1))
            rs = rs[:, :LANES] + rs[:, LANES:2 * LANES] + rs[:, 2 * LANES:]
            cum = rs[0:c]
            rev = rs[c:2 * c]
            total = cum[c - 1:c]
            st = st_ref[h]
            vb = v.astype(BF16)
            inter = _dot_nt((q * jnp.exp(cum)).astype(BF16), st.astype(BF16))
            scores = jnp.where(eye, _dot_nt(q.astype(BF16), k.astype(BF16)), 0.0)
            for li in range(len(levels)):
                e = jnp.exp(-jnp.abs(rs[(2 + li) * c:(3 + li) * c]))
                ql = jnp.where(upper[li], q * e, 0.0).astype(BF16)
                kl = jnp.where(upper[li], 0.0, k * e).astype(BF16)
                scores = scores + jnp.where(same[li], _dot_nt(ql, kl), 0.0)
            o = inter + _dot(scores.astype(BF16), vb)
            kdec = (k * jnp.exp(rev)).astype(BF16)
            st_ref[h] = st * jnp.exp(total) + _dot_tn(vb, kdec)
            ms = jnp.mean(o * o, axis=-1, keepdims=True)
            on = o * lax.rsqrt(ms + EPS) * nw_ref[h] * (zg * _sigmoid(zg))
            o_ref[h, pl.ds(r0, c), :] = on.astype(o_ref.dtype)
        return carry

    lax.fori_loop(0, n_chunks, one_chunk, 0)

    @pl.when(n == pl.num_programs(1) - 1)
    def _():
        for h in range(N_HA):
            sout_ref[0, h] = st_ref[h].T


def _hgrn(z_a, s0, lb, nw, *, n_seq, rows_per_seq, chunk, valid, rows_per_step):
    m = z_a.shape[1]
    assert m == n_seq * rows_per_seq and rows_per_seq % rows_per_step == 0 and rows_per_step % chunk == 0
    nb = rows_per_seq // rows_per_step
    a_mat = jnp.asarray(_hgrn_matrix(chunk), BF16)
    zspec = lambda s: pl.BlockSpec((N_HA, rows_per_step, LANES), lambda b, n: (s, b * nb + n, 0))
    sspec = pl.BlockSpec((1, N_HA, LANES, LANES), lambda b, n: (b, 0, 0, 0))
    vspec = pl.BlockSpec((N_HA, 1, LANES), lambda b, n: (0, 0, 0))
    return pl.pallas_call(
        functools.partial(_hgrn_kernel, chunk=chunk, valid=valid, n_chunks=rows_per_step // chunk),
        grid=(n_seq, nb),
        in_specs=[zspec(0), zspec(1), zspec(2), zspec(3), sspec, vspec, vspec,
                  pl.BlockSpec(a_mat.shape, lambda b, n: (0, 0))],
        out_specs=[pl.BlockSpec((N_HA, rows_per_step, LANES), lambda b, n: (0, b * nb + n, 0)), sspec],
        out_shape=[jax.ShapeDtypeStruct((N_HA, m, LANES), BF16),
                   jax.ShapeDtypeStruct((n_seq, N_HA, LANES, LANES), F32)],
        scratch_shapes=[pltpu.VMEM((N_HA, LANES, LANES), F32)],
        compiler_params=_params("parallel", "arbitrary"),
        name="hgrn",
    )(z_a, z_a, z_a, z_a, s0, lb, nw, a_mat)


ATT_TILE = 2048
ATT_BATCH = 4


def _band_softmax(q, kp, ko, vp, vo, bias_p, bias_o, scale):
    qb = (q * scale).astype(BF16)
    ones = jnp.ones(vp.shape, BF16)
    sp = _dot_nt(qb, kp.astype(BF16)) + bias_p
    so = _dot_nt(qb, ko.astype(BF16)) + bias_o
    m = jnp.maximum(jnp.max(sp, axis=-1, keepdims=True), jnp.max(so, axis=-1, keepdims=True))
    pp = jnp.exp(sp - m).astype(BF16)
    po = jnp.exp(so - m).astype(BF16)
    acc = (_dot(pp, jnp.concatenate([vp.astype(BF16), ones], axis=1))
           + _dot(po, jnp.concatenate([vo.astype(BF16), ones], axis=1)))
    den = acc[:, LANES:]
    return acc[:, :LANES] / den, m + jnp.log(den)


def _attn_prompt_kernel(slopes_ref, *refs):
    ins, o_ref, scr, bias_ref = refs[:15], refs[15], refs[16:22], refs[22]
    h = pl.program_id(0)
    n = pl.program_id(1)
    blk = NKEYS
    scale = float(LANES) ** -0.5
    ti = lax.broadcasted_iota(jnp.int32, (blk, blk), 0)
    si = lax.broadcasted_iota(jnp.int32, (blk, blk), 1)
    dist_o = (ti - si).astype(F32)
    dist_p = dist_o + float(blk)
    for g, (_, dil) in enumerate(PATTERNS):
        q_ref, ko_ref, kp_ref, vo_ref, vp_ref = ins[5 * g:5 * g + 5]
        og_ref, lg_ref = scr[2 * g], scr[2 * g + 1]
        slope_d = slopes_ref[g, h] * float(dil)
        span = blk * dil
        nj = ATT_TILE // span
        bias_prev = jnp.where(dist_p <= float(blk), -slope_d * dist_p, NEG)
        bias_ref[3 * g] = jnp.where(dist_o >= 0.0, -slope_d * dist_o, NEG)
        bias_ref[3 * g + 1] = bias_prev
        bias_ref[3 * g + 2] = jnp.where(n > 0, bias_prev, NEG)

        def unit(start, prev_start, from_prev_tile, g=g, dil=dil, refs5=(q_ref, ko_ref, kp_ref, vo_ref, vp_ref),
                 outs=(og_ref, lg_ref)):
            q_ref, ko_ref, kp_ref, vo_ref, vp_ref = refs5
            rows = pl.ds(start, blk, stride=dil) if dil > 1 else pl.ds(start, blk)
            prows = pl.ds(prev_start, blk, stride=dil) if dil > 1 else pl.ds(prev_start, blk)
            kp = kp_ref[0, prows, :] if from_prev_tile else ko_ref[0, prows, :]
            vp = vp_ref[0, prows, :] if from_prev_tile else vo_ref[0, prows, :]
            bias_p = bias_ref[3 * g + 2] if from_prev_tile else bias_ref[3 * g + 1]
            o, lse = _band_softmax(q_ref[0, rows, :], kp, ko_ref[0, rows, :], vp, vo_ref[0, rows, :],
                                   bias_p, bias_ref[3 * g], scale)
            outs[0][rows, :] = o
            outs[1][rows, :] = lse

        n_batches = nj * dil // ATT_BATCH

        def batch(b, first_tile_block):
            for i in range(ATT_BATCH):
                u = b * ATT_BATCH + i
                j, r = u // dil, u % dil
                if first_tile_block:
                    unit(r, r, True)
                else:
                    unit(j * span + r, (j - 1) * span + r, False)

        b0 = 0
        if dil < ATT_BATCH:
            for u in range(ATT_BATCH):
                j, r = u // dil, u % dil
                if j == 0:
                    unit(r, r, True)
                else:
                    unit(j * span + r, (j - 1) * span + r, False)
            b0 = 1
        n_first = max(dil // ATT_BATCH, b0)

        def first_batches(b, carry):
            batch(b, True)
            return carry

        def later_batches(b, carry):
            batch(b, False)
            return carry

        if n_first > b0:
            lax.fori_loop(b0, n_first, first_batches, 0)
        if n_batches > n_first:
            lax.fori_loop(n_first, n_batches, later_batches, 0)

    rows_per_pass = 256

    def merge(c, carry):
        rows = pl.ds(pl.multiple_of(c * rows_per_pass, rows_per_pass), rows_per_pass)
        l0, l1, l2 = scr[1][rows, :], scr[3][rows, :], scr[5][rows, :]
        mx = jnp.maximum(jnp.maximum(l0, l1), l2)
        w0, w1, w2 = jnp.exp(l0 - mx), jnp.exp(l1 - mx), jnp.exp(l2 - mx)
        num = w0 * scr[0][rows, :] + w1 * scr[2][rows, :] + w2 * scr[4][rows, :]
        o_ref[rows, :] = (num / (w0 + w1 + w2)).astype(o_ref.dtype)
        return carry

    lax.fori_loop(0, ATT_TILE // rows_per_pass, merge, 0)


def _attn_prompt(z, slopes):
    t = z.shape[1]
    assert t % ATT_TILE == 0
    in_specs = [pl.BlockSpec(memory_space=pltpu.SMEM)]
    args = [slopes]
    for g, (win, dil) in enumerate(PATTERNS):
        assert win // dil == NKEYS
        prev_rows = NKEYS * dil
        ratio = ATT_TILE // prev_rows
        own = lambda base, g=g: pl.BlockSpec((1, ATT_TILE, LANES), lambda h, n: (base + g * HPG + h, n, 0))
        prev = lambda base, g=g, ratio=ratio, prev_rows=prev_rows: pl.BlockSpec(
            (1, prev_rows, LANES), lambda h, n: (base + g * HPG + h, jnp.maximum(n * ratio - 1, 0), 0))
        in_specs += [own(SLOT_Q), own(SLOT_K), prev(SLOT_K), own(SLOT_V), prev(SLOT_V)]
        args += [z] * 5
    return pl.pallas_call(
        _attn_prompt_kernel,
        grid=(HPG, t // ATT_TILE),
        in_specs=in_specs,
        out_specs=pl.BlockSpec((ATT_TILE, LANES), lambda h, n: (n, h)),
        out_shape=jax.ShapeDtypeStruct((t, HPG * LANES), BF16),
        scratch_shapes=[pltpu.VMEM((ATT_TILE, LANES), F32)] * 6 + [pltpu.VMEM((9, NKEYS, NKEYS), F32)],
        compiler_params=_params("parallel", "arbitrary"),
        name="attn_prompt",
    )(*args)


def _attn_sample_kernel(slopes_ref, z_ref, c0_ref, c1_ref, c2_ref, o_ref, *scr, n_new):
    caches = (c0_ref, c1_ref, c2_ref)
    bufs, sem = scr[:3], scr[3]
    b = pl.program_id(0)
    s = n_new
    scale = float(LANES) ** -0.5

    def copies():
        out = []
        for g in range(N_GROUPS):
            for kv in range(2):
                for h in range(HPG):
                    out.append(pltpu.make_async_copy(caches[g].at[0, b, :, kv, h, :], bufs[g].at[kv, h],
                                                     sem.at[g, kv, h]))
        return out

    for cp in copies():
        cp.start()
    for cp in copies():
        cp.wait()

    for h in range(HPG):
        outs, lses = [], []
        for g, (win, dil) in enumerate(PATTERNS):
            length = bufs[g].shape[2]
            q = z_ref[g * HPG + h, 0]
            kn = z_ref[12 + g * HPG + h, 0]
            vn = z_ref[24 + g * HPG + h, 0]
            slope = slopes_ref[g, h]
            qb = q.astype(BF16)
            qi = lax.broadcasted_iota(jnp.int32, (s, length), 0)
            ri = lax.broadcasted_iota(jnp.int32, (s, length), 1)
            dist = length + qi - ri
            ok = (dist <= win) & ((dist & (dil - 1)) == 0)
            sc = _dot_nt(qb, bufs[g][0, h].astype(BF16)) * scale - slope * dist.astype(F32)
            sc = jnp.where(ok, sc, NEG)
            qn = lax.broadcasted_iota(jnp.int32, (s, s), 0)
            rn = lax.broadcasted_iota(jnp.int32, (s, s), 1)
            dn = qn - rn
            okn = (dn >= 0) & ((dn & (dil - 1)) == 0)
            sn = _dot_nt(qb, kn.astype(BF16)) * scale - slope * dn.astype(F32)
            sn = jnp.where(okn, sn, NEG)
            m = jnp.maximum(jnp.max(sc, axis=-1, keepdims=True), jnp.max(sn, axis=-1, keepdims=True))
            pc = jnp.exp(sc - m)
            pn = jnp.exp(sn - m)
            den = jnp.sum(pc, axis=-1, keepdims=True) + jnp.sum(pn, axis=-1, keepdims=True)
            acc = _dot(pc.astype(BF16), bufs[g][1, h].astype(BF16)) + _dot(pn.astype(BF16), vn.astype(BF16))
            outs.append(acc / den)
            lses.append(m + jnp.log(den))
        mx = jnp.maximum(jnp.maximum(lses[0], lses[1]), lses[2])
        ws = [jnp.exp(l - mx) for l in lses]
        num = ws[0] * outs[0] + ws[1] * outs[1] + ws[2] * outs[2]
        o_ref[0, :, h * LANES:(h + 1) * LANES] = num / (ws[0] + ws[1] + ws[2])


def _attn_sample(z_qkv, caches, slopes, n_seq, n_new):
    z4 = z_qkv.reshape(3 * N_GROUPS * HPG, n_seq, n_new, LANES)
    for g, (win, _) in enumerate(PATTERNS):
        assert caches[g].shape[2] == win, "window buffers shorter than the window are not supported"
    bufs = [pltpu.VMEM((2, HPG, c.shape[2], LANES), F32) for c in caches]
    return pl.pallas_call(
        functools.partial(_attn_sample_kernel, n_new=n_new),
        grid=(n_seq,),
        in_specs=[pl.BlockSpec(memory_space=pltpu.SMEM),
                  pl.BlockSpec((z4.shape[0], 1, n_new, LANES), lambda b: (0, b, 0, 0)),
                  pl.BlockSpec(memory_space=pl.ANY), pl.BlockSpec(memory_space=pl.ANY),
                  pl.BlockSpec(memory_space=pl.ANY)],
        out_specs=pl.BlockSpec((1, n_new, HPG * LANES), lambda b: (b, 0, 0)),
        out_shape=jax.ShapeDtypeStruct((n_seq, n_new, HPG * LANES), F32),
        scratch_shapes=bufs + [pltpu.SemaphoreType.DMA((N_GROUPS, 2, HPG))],
        compiler_params=_params("arbitrary"),
        name="attn_sample",
    )(slopes, z4, *caches)


def _mix_kernel(oa_ref, ob_ref, ga_ref, gb_ref, x_ref, wa_ref, wb_ref, wo_ref, o_ref):
    oa = jnp.concatenate([oa_ref[h] for h in range(oa_ref.shape[0])], axis=1)
    ya = _dot(oa, wa_ref[...])
    yb = _dot(ob_ref[...], wb_ref[...])
    parts = []
    for c in range(ga_ref.shape[0]):
        cols = slice(c * LANES, (c + 1) * LANES)
        mix = _sigmoid(ga_ref[c].astype(F32)) * ya[:, cols] + _sigmoid(gb_ref[c].astype(F32)) * yb[:, cols]
        parts.append(mix.astype(BF16))
    o_ref[...] = x_ref[...] + _dot(jnp.concatenate(parts, axis=1), wo_ref[...])


def _mix(o_a, o_b, z_gate, x, w_a, w_b, w_o, tm):
    m, d = x.shape
    ng = d // LANES
    assert m % tm == 0
    full = lambda a: pl.BlockSpec(a.shape, lambda i: (0, 0))
    return pl.pallas_call(
        _mix_kernel,
        grid=(m // tm,),
        in_specs=[pl.BlockSpec((N_HA, tm, LANES), lambda i: (0, i, 0)),
                  pl.BlockSpec((tm, o_b.shape[1]), lambda i: (i, 0)),
                  pl.BlockSpec((ng, tm, LANES), lambda i: (SLOT_GATE // ng, i, 0)),
                  pl.BlockSpec((ng, tm, LANES), lambda i: (SLOT_GATE // ng + 1, i, 0)),
                  pl.BlockSpec((tm, d), lambda i: (i, 0)),
                  full(w_a), full(w_b), full(w_o)],
        out_specs=pl.BlockSpec((tm, d), lambda i: (i, 0)),
        out_shape=jax.ShapeDtypeStruct((m, d), F32),
        compiler_params=_params("parallel"),
        name="mix",
    )(o_a, o_b, z_gate, z_gate, x, w_a, w_b, w_o)


def _ffn_kernel(*refs, tm, tf, seq_rows):
    (x_ref, nw_ref, wug_ref, wua_ref, cwg_ref, cwa_ref, cbg_ref, cba_ref, wd_ref, fw_ref) = refs[:10]
    if seq_rows is None:
        y_ref, sg_ref, sa_ref, h_ref, acc_ref, ug_ref, ua_ref, carry_ref = refs[10:]
    else:
        pg_ref, pa_ref, y_ref, sg_ref, sa_ref, h_ref, acc_ref = refs[10:]
    i = pl.program_id(0)
    j = pl.program_id(1)

    @pl.when(j == 0)
    def _():
        x = x_ref[...]
        ms = jnp.mean(x * x, axis=-1, keepdims=True)
        h_ref[...] = (x * lax.rsqrt(ms + EPS) * nw_ref[...]).astype(BF16)
        acc_ref[...] = jnp.zeros_like(acc_ref)

    if seq_rows is None:
        @pl.when((i == 0) & (j == 0))
        def _():
            carry_ref[...] = jnp.zeros_like(carry_ref)

    def conv(part, wu_ref, cw_ref, cb_ref):
        u = _dot(h_ref[...], wu_ref[...])
        if seq_rows is None:
            ub_ref, s_ref = ((ug_ref, sg_ref), (ua_ref, sa_ref))[part]
            cols = pl.ds(pl.multiple_of(j * tf, tf), tf)
            ub_ref[pl.ds(0, 8), :] = carry_ref[part, :, cols]
            ub_ref[pl.ds(8, tm), :] = u
            u1 = ub_ref[pl.ds(7, tm), :]
            u2 = ub_ref[pl.ds(6, tm), :]
            carry_ref[part, :, cols] = ub_ref[pl.ds(tm, 8), :]
            s_ref[0] = ub_ref[pl.ds(tm + 6, 2), :]
        else:
            p_ref, s_ref = ((pg_ref, sg_ref), (pa_ref, sa_ref))[part]
            pos = lax.broadcasted_iota(jnp.int32, (tm, 1), 0) % seq_rows
            u1 = jnp.where(pos == 0, p_ref[1], pltpu.roll(u, 1, 0))
            u2 = jnp.where(pos == 0, p_ref[0], jnp.where(pos == 1, p_ref[1], pltpu.roll(u, 2, 0)))
            s_ref[...] = u
        return cb_ref[...] + cw_ref[2:3, :] * u + cw_ref[1:2, :] * u1 + cw_ref[0:1, :] * u2

    cg = conv(0, wug_ref, cwg_ref, cbg_ref)
    ca = conv(1, wua_ref, cwa_ref, cba_ref)
    act = (cg * _sigmoid(cg) * ca).astype(BF16)
    acc_ref[...] += _dot(act, wd_ref[...])

    @pl.when(j == pl.num_programs(1) - 1)
    def _():
        x2 = x_ref[...] + acc_ref[...]
        ms = jnp.mean(x2 * x2, axis=-1, keepdims=True)
        y_ref[...] = x2 * lax.rsqrt(ms + EPS) * fw_ref[...]


def _ffn(x1, nw, w_up, conv_w, conv_b, w_down, fw, *, tm, tf, seq_rows=None, prev=None):
    m, d = x1.shape
    dff = w_down.shape[0]
    assert m % tm == 0 and dff % tf == 0
    nf = dff // tf
    row = lambda i, j: (i, 0)
    const = lambda i, j: (0, 0)
    gcol = lambda i, j: (0, j)
    acol = lambda i, j: (0, nf + j)
    in_specs = [pl.BlockSpec((tm, d), row), pl.BlockSpec((1, d), const),
                pl.BlockSpec((d, tf), gcol), pl.BlockSpec((d, tf), acol),
                pl.BlockSpec((3, tf), gcol), pl.BlockSpec((3, tf), acol),
                pl.BlockSpec((1, tf), gcol), pl.BlockSpec((1, tf), acol),
                pl.BlockSpec((tf, d), lambda i, j: (j, 0)), pl.BlockSpec((1, d), const)]
    args = [x1, nw, w_up, w_up, conv_w, conv_w, conv_b, conv_b, w_down, fw]
    scratch = [pltpu.VMEM((tm, d), BF16), pltpu.VMEM((tm, d), F32)]
    if seq_rows is None:
        state = jax.ShapeDtypeStruct((m // tm, 2, dff), F32)
        state_spec = pl.BlockSpec((1, 2, tf), lambda i, j: (i, 0, j))
        scratch += [pltpu.VMEM((tm + 8, tf), F32), pltpu.VMEM((tm + 8, tf), F32),
                    pltpu.VMEM((2, 8, dff), F32)]
    else:
        assert m // tm == 1 and tm % seq_rows == 0
        in_specs += [pl.BlockSpec((2, tm, tf), lambda i, j: (0, 0, j)),
                     pl.BlockSpec((2, tm, tf), lambda i, j: (0, 0, nf + j))]
        args += [prev, prev]
        state = jax.ShapeDtypeStruct((m, dff), F32)
        state_spec = pl.BlockSpec((tm, tf), lambda i, j: (i, j))
    return pl.pallas_call(
        functools.partial(_ffn_kernel, tm=tm, tf=tf, seq_rows=seq_rows),
        grid=(m // tm, nf),
        in_specs=in_specs,
        out_specs=[pl.BlockSpec((tm, d), row), state_spec, state_spec],
        out_shape=[jax.ShapeDtypeStruct((m, d), F32), state, state],
        scratch_shapes=scratch,
        compiler_params=_params("arbitrary", "arbitrary"),
        name="ffn",
    )(*args)


def _alibi_slopes():
    n = N_GROUPS * HPG
    s = 2.0 ** (-8.0 * np.arange(1, n + 1, dtype=np.float32) / n)
    return jnp.asarray(s, F32).reshape(N_GROUPS, HPG)


def _kv_rows(z, g, rows):
    k = z[SLOT_K + g * HPG:SLOT_K + (g + 1) * HPG, rows]
    v = z[SLOT_V + g * HPG:SLOT_V + (g + 1) * HPG, rows]
    return jnp.stack([k, v], axis=0).transpose(2, 0, 1, 3)


def kernel(x_prompt, x_sample, state_hgrn, cache_kv_g0, cache_kv_g1, cache_kv_g2, state_ffn_conv,
           norm_mix_w, w_in, lb_logits, hgrn_norm_w, w_proj_a, w_proj_b, w_out, norm_ffn_w,
           w_up, ffn_conv_w, ffn_conv_b, w_down, norm_final_w):
    assert w_in.shape[0] == 1, "single-layer trunk"
    _, t, d = x_prompt.shape
    n_seq, n_new, _ = x_sample.shape
    dff = w_down.shape[1]
    slopes = _alibi_slopes()
    lb = jnp.cumsum(jax.nn.softmax(lb_logits.astype(F32), axis=0), axis=0)[0].reshape(N_HA, 1, LANES)
    hnw = hgrn_norm_w[0].reshape(N_HA, 1, LANES)
    nmw = norm_mix_w[0].reshape(1, d)
    nfw = norm_ffn_w[0].reshape(1, d)
    fw = norm_final_w.reshape(1, d)
    n_a, n_qkv = 4 * N_HA * LANES, 3 * N_GROUPS * HPG * LANES
    w_in_b = jnp.concatenate([w_in[0, :, :n_a], w_in[0, :, n_a + n_qkv:], w_in[0, :, n_a:n_a + n_qkv]],
                             axis=1).astype(BF16)
    w_a_b = w_proj_a[0].astype(BF16)
    w_b_b = w_proj_b[0].astype(BF16)
    w_o_b = w_out[0].astype(BF16)
    w_up_b = w_up[0].astype(BF16)
    w_dn_b = w_down[0].astype(BF16)
    cw = ffn_conv_w[0]
    cb = ffn_conv_b[0].reshape(1, 2 * dff)
    caches = (cache_kv_g0, cache_kv_g1, cache_kv_g2)

    xp = x_prompt[0]
    z = _inproj(xp, nmw, w_in_b, 1024, 1280)
    o_a, s_p = _hgrn(z, jnp.zeros((1, N_HA, LANES, LANES), F32), lb, hnw,
                     n_seq=1, rows_per_seq=t, chunk=64, valid=64, rows_per_step=512)
    o_b = _attn_prompt(z, slopes)
    x1 = _mix(o_a, o_b, z, xp, w_a_b, w_b_b, w_o_b, 256)
    y_p, sg, sa = _ffn(x1, nfw, w_up_b, cw, cb, w_dn_b, fw, tm=512, tf=512)
    conv_p = jnp.concatenate([sg[-1], sa[-1]], axis=1)[None, None]
    kv_p = [_kv_rows(z, g, slice(t - min(win, t), t))[None, None] for g, (win, _) in enumerate(PATTERNS)]

    ms = n_seq * n_new
    xs = x_sample.reshape(ms, d)
    z = _inproj(xs, nmw, w_in_b, ms, 1280)
    pad = 8
    z_a_pad = jnp.pad(z[:SLOT_GATE].reshape(SLOT_GATE, n_seq, n_new, LANES),
                      ((0, 0), (0, 0), (0, pad - n_new), (0, 0)))
    o_a, s_s = _hgrn(z_a_pad.reshape(SLOT_GATE, n_seq * pad, LANES), state_hgrn[0], lb, hnw,
                     n_seq=n_seq, rows_per_seq=pad, chunk=pad, valid=n_new, rows_per_step=pad)
    o_a = o_a.reshape(N_HA, n_seq, pad, LANES)[:, :, :n_new].reshape(N_HA, ms, LANES)
    o_b = _attn_sample(z[SLOT_Q:], caches, slopes, n_seq, n_new).reshape(ms, HPG * LANES).astype(BF16)
    x1 = _mix(o_a, o_b, z, xs, w_a_b, w_b_b, w_o_b, ms)
    prev = jnp.repeat(state_ffn_conv[0].transpose(1, 0, 2), n_new, axis=1)
    y_s, ug, ua = _ffn(x1, nfw, w_up_b, cw, cb, w_dn_b, fw, tm=ms, tf=512, seq_rows=n_new, prev=prev)
    u = jnp.concatenate([ug, ua], axis=1).reshape(n_seq, n_new, 2 * dff)
    conv_s = u[:, n_new - 2:][None]
    kv_s = []
    for g in range(N_GROUPS):
        new = _kv_rows(z, g, slice(0, ms)).reshape(n_seq, n_new, 2, HPG, LANES)
        length = caches[g].shape[2]
        kv_s.append(jnp.concatenate([caches[g][0], new], axis=1)[:, -length:][None])

    return (y_p[None], y_s.reshape(n_seq, n_new, d), s_p[None], s_s[None],
            kv_p[0], kv_s[0], kv_p[1], kv_s[1], kv_p[2], kv_s[2], conv_p, conv_s)
```

```python
import functools

import numpy as np
import jax
import jax.numpy as jnp
from jax import lax
from jax.experimental import pallas as pl
from jax.experimental.pallas import tpu as pltpu

F32 = jnp.float32
BF16 = jnp.bfloat16

LANES = 128
EPS = 1e-6
NEG = -1e30
VMEM_LIMIT = 56 * 1024 * 1024

N_HA = 8
N_GROUPS = 3
HPG = 4
PATTERNS = ((128, 1), (512, 4), (2048, 16))
NKEYS = 128
SLOT_A = 0
SLOT_GATE = 32
SLOT_Q = 64
SLOT_K = 76
SLOT_V = 88


def _dot(a, b):
    return jnp.dot(a, b, preferred_element_type=F32)


def _dot_nt(a, b):
    return lax.dot_general(a, b, (((1,), (1,)), ((), ())), preferred_element_type=F32)


def _dot_tn(a, b):
    return lax.dot_general(a, b, (((0,), (0,)), ((), ())), preferred_element_type=F32)


def _sigmoid(x):
    return 1.0 / (1.0 + jnp.exp(-x))


def _params(*sem):
    return pltpu.CompilerParams(dimension_semantics=sem, vmem_limit_bytes=VMEM_LIMIT)


def _inproj_kernel(x_ref, nw_ref, w_ref, o_ref, h_ref):
    @pl.when(pl.program_id(1) == 0)
    def _():
        x = x_ref[...]
        ms = jnp.mean(x * x, axis=-1, keepdims=True)
        h_ref[...] = (x * lax.rsqrt(ms + EPS) * nw_ref[...]).astype(BF16)

    z = _dot(h_ref[...], w_ref[...])
    for c in range(o_ref.shape[0]):
        o_ref[c] = z[:, c * LANES:(c + 1) * LANES].astype(o_ref.dtype)


def _inproj(x, nw, w, tm, tn):
    m, d = x.shape
    n_slots = w.shape[1] // LANES
    spt = tn // LANES
    assert m % tm == 0 and n_slots % spt == 0
    return pl.pallas_call(
        _inproj_kernel,
        grid=(m // tm, n_slots // spt),
        in_specs=[
            pl.BlockSpec((tm, d), lambda i, j: (i, 0)),
            pl.BlockSpec((1, d), lambda i, j: (0, 0)),
            pl.BlockSpec((d, tn), lambda i, j: (0, j)),
        ],
        out_specs=pl.BlockSpec((spt, tm, LANES), lambda i, j: (j, i, 0)),
        out_shape=jax.ShapeDtypeStruct((n_slots, m, LANES), F32),
        scratch_shapes=[pltpu.VMEM((tm, d), BF16)],
        compiler_params=_params("parallel", "arbitrary"),
        name="inproj",
    )(x, nw, w)


HGRN_HEAD_BATCH = 8


def _hgrn_levels(c):
    levels, h = [], c // 2
    while h >= 1:
        levels.append(h)
        h //= 2
    return levels


def _hgrn_matrix(c):
    r = np.arange(c)
    low = (r[None, :] <= r[:, None]).astype(np.float32)
    mats = [low, 1.0 - low]
    for h in _hgrn_levels(c)[:-1]:
        mid = (r // (2 * h)) * (2 * h) + h
        ref = (r[None, :] <= (mid - 1)[:, None]).astype(np.float32)
        mats.append(low - ref)
    return np.concatenate(mats, axis=0)


def _hgrn_kernel(zq_ref, zf_ref, zi_ref, zg_ref, s0_ref, lb_ref, nw_ref, a_ref, o_ref, sout_ref,
                 st_ref, *, chunk, valid, n_chunks):
    c = chunk
    levels = _hgrn_levels(c)
    n = pl.program_id(1)

    @pl.when(n == 0)
    def _():
        for h in range(N_HA):
            st_ref[h] = s0_ref[0, h].T

    row = lax.broadcasted_iota(jnp.int32, (c, 1), 0)
    ti = lax.broadcasted_iota(jnp.int32, (c, c), 0)
    si = lax.broadcasted_iota(jnp.int32, (c, c), 1)
    eye = ti == si
    same = [(ti // (2 * h)) == (si // (2 * h)) for h in levels]
    upper = [(row % (2 * h)) >= h for h in levels]
    a_mat = a_ref[...]

    n_hb = N_HA // HGRN_HEAD_BATCH

    def one_chunk(it, carry):
        r0 = pl.multiple_of((it // n_hb) * c, c)
        for h in ((it % n_hb) * HGRN_HEAD_BATCH + hh for hh in range(HGRN_HEAD_BATCH)):
            zq = zq_ref[h, pl.ds(r0, c), :].astype(F32)
            zf = zf_ref[h, pl.ds(r0, c), :].astype(F32)
            v = zi_ref[h, pl.ds(r0, c), :].astype(F32)
            zg = zg_ref[h, pl.ds(r0, c), :].astype(F32)
            lb = lb_ref[h]
            f = lb + (1.0 - lb) * _sigmoid(zf)
            logf = jnp.log(f)
            k = 1.0 - f
            if valid < c:
                live = row < valid
                f = jnp.where(live, f, 1.0)
                logf = jnp.where(live, logf, 0.0)
                k = jnp.where(live, k, 0.0)
            q = zq * _sigmoid(zq)
            hi = logf.astype(BF16)
            lo = (logf - hi.astype(F32)).astype(BF16)
            rs = _dot(a_mat, jnp.concatenate([hi, lo], axis=1))
            rs = rs[:, :LANES] + rs[:, LANES:]
            cum = rs[0:c]
            rev = rs[c:2 * c]
            total = cum[c - 1:c]
            st = st_ref[h]
            vb = v.astype(BF16)
            inter = _dot_nt((q * jnp.exp(cum)).astype(BF16), st.astype(BF16))
            scores = jnp.where(eye, _dot_nt(q.astype(BF16), k.astype(BF16)), 0.0)
            for li in range(len(levels)):
                if levels[li] == 1:
                    qe, ke = q * f, k
                else:
                    e = jnp.exp(-jnp.abs(rs[(2 + li) * c:(3 + li) * c]))
                    qe, ke = q * e, k * e
                ql = jnp.where(upper[li], qe, 0.0).astype(BF16)
                kl = jnp.where(upper[li], 0.0, ke).astype(BF16)
                scores = scores + jnp.where(same[li], _dot_nt(ql, kl), 0.0)
            o = inter + _dot(scores.astype(BF16), vb)
            kdec = (k * jnp.exp(rev)).astype(BF16)
            st_ref[h] = st * jnp.exp(total) + _dot_tn(vb, kdec)
            ms = jnp.mean(o * o, axis=-1, keepdims=True)
            on = o * lax.rsqrt(ms + EPS) * nw_ref[h] * (zg * _sigmoid(zg))
            o_ref[h, pl.ds(r0, c), :] = on.astype(o_ref.dtype)
        return carry

    lax.fori_loop(0, n_chunks * n_hb, one_chunk, 0)

    @pl.when(n == pl.num_programs(1) - 1)
    def _():
        for h in range(N_HA):
            sout_ref[0, h] = st_ref[h].T


def _hgrn(z_a, s0, lb, nw, *, n_seq, rows_per_seq, chunk, valid, rows_per_step):
    m = z_a.shape[1]
    assert m == n_seq * rows_per_seq and rows_per_seq % rows_per_step == 0 and rows_per_step % chunk == 0
    nb = rows_per_seq // rows_per_step
    a_mat = jnp.asarray(_hgrn_matrix(chunk), BF16)
    zspec = lambda s: pl.BlockSpec((N_HA, rows_per_step, LANES), lambda b, n: (s, b * nb + n, 0))
    sspec = pl.BlockSpec((1, N_HA, LANES, LANES), lambda b, n: (b, 0, 0, 0))
    vspec = pl.BlockSpec((N_HA, 1, LANES), lambda b, n: (0, 0, 0))
    return pl.pallas_call(
        functools.partial(_hgrn_kernel, chunk=chunk, valid=valid, n_chunks=rows_per_step // chunk),
        grid=(n_seq, nb),
        in_specs=[zspec(0), zspec(1), zspec(2), zspec(3), sspec, vspec, vspec,
                  pl.BlockSpec(a_mat.shape, lambda b, n: (0, 0))],
        out_specs=[pl.BlockSpec((N_HA, rows_per_step, LANES), lambda b, n: (0, b * nb + n, 0)), sspec],
        out_shape=[jax.ShapeDtypeStruct((N_HA, m, LANES), BF16),
                   jax.ShapeDtypeStruct((n_seq, N_HA, LANES, LANES), F32)],
        scratch_shapes=[pltpu.VMEM((N_HA, LANES, LANES), F32)],
        compiler_params=_params("parallel", "arbitrary"),
        name="hgrn",
    )(z_a, z_a, z_a, z_a, s0, lb, nw, a_mat)


ATT_TILE = 2048
ATT_BATCH = 4


def _band_softmax(q, kp, ko, vp, vo, bias_p, bias_o, scale):
    qb = (q * scale).astype(BF16)
    ones = jnp.ones(vp.shape, BF16)
    sp = _dot_nt(qb, kp.astype(BF16)) + bias_p
    so = _dot_nt(qb, ko.astype(BF16)) + bias_o
    m = jnp.maximum(jnp.max(sp, axis=-1, keepdims=True), jnp.max(so, axis=-1, keepdims=True))
    pp = jnp.exp(sp - m).astype(BF16)
    po = jnp.exp(so - m).astype(BF16)
    acc = (_dot(pp, jnp.concatenate([vp.astype(BF16), ones], axis=1))
           + _dot(po, jnp.concatenate([vo.astype(BF16), ones], axis=1)))
    den = acc[:, LANES:]
    return acc[:, :LANES] / den, m + jnp.log(den)


def _attn_prompt_kernel(slopes_ref, *refs):
    ins, o_ref, scr, bias_ref = refs[:15], refs[15], refs[16:22], refs[22]
    h = pl.program_id(0)
    n = pl.program_id(1)
    blk = NKEYS
    scale = float(LANES) ** -0.5
    ti = lax.broadcasted_iota(jnp.int32, (blk, blk), 0)
    si = lax.broadcasted_iota(jnp.int32, (blk, blk), 1)
    dist_o = (ti - si).astype(F32)
    dist_p = dist_o + float(blk)
    for g, (_, dil) in enumerate(PATTERNS):
        q_ref, ko_ref, kp_ref, vo_ref, vp_ref = ins[5 * g:5 * g + 5]
        og_ref, lg_ref = scr[2 * g], scr[2 * g + 1]
        slope_d = slopes_ref[g, h] * float(dil)
        span = blk * dil
        nj = ATT_TILE // span
        bias_prev = jnp.where(dist_p <= float(blk), -slope_d * dist_p, NEG)
        bias_ref[3 * g] = jnp.where(dist_o >= 0.0, -slope_d * dist_o, NEG)
        bias_ref[3 * g + 1] = bias_prev
        bias_ref[3 * g + 2] = jnp.where(n > 0, bias_prev, NEG)

        def unit(start, prev_start, from_prev_tile, g=g, dil=dil, refs5=(q_ref, ko_ref, kp_ref, vo_ref, vp_ref),
                 outs=(og_ref, lg_ref)):
            q_ref, ko_ref, kp_ref, vo_ref, vp_ref = refs5
            rows = pl.ds(start, blk, stride=dil) if dil > 1 else pl.ds(start, blk)
            prows = pl.ds(prev_start, blk, stride=dil) if dil > 1 else pl.ds(prev_start, blk)
            kp = kp_ref[0, prows, :] if from_prev_tile else ko_ref[0, prows, :]
            vp = vp_ref[0, prows, :] if from_prev_tile else vo_ref[0, prows, :]
            bias_p = bias_ref[3 * g + 2] if from_prev_tile else bias_ref[3 * g + 1]
            o, lse = _band_softmax(q_ref[0, rows, :], kp, ko_ref[0, rows, :], vp, vo_ref[0, rows, :],
                                   bias_p, bias_ref[3 * g], scale)
            outs[0][rows, :] = o
            outs[1][rows, :] = lse

        n_batches = nj * dil // ATT_BATCH

        def batch(b, first_tile_block):
            for i in range(ATT_BATCH):
                u = b * ATT_BATCH + i
                j, r = u // dil, u % dil
                if first_tile_block:
                    unit(r, r, True)
                else:
                    unit(j * span + r, (j - 1) * span + r, False)

        b0 = 0
        if dil < ATT_BATCH:
            for u in range(ATT_BATCH):
                j, r = u // dil, u % dil
                if j == 0:
                    unit(r, r, True)
                else:
                    unit(j * span + r, (j - 1) * span + r, False)
            b0 = 1
        n_first = max(dil // ATT_BATCH, b0)

        def first_batches(b, carry):
            batch(b, True)
            return carry

        def later_batches(b, carry):
            batch(b, False)
            return carry

        if n_first > b0:
            lax.fori_loop(b0, n_first, first_batches, 0)
        if n_batches > n_first:
            lax.fori_loop(n_first, n_batches, later_batches, 0)

    rows_per_pass = 256

    def merge(c, carry):
        rows = pl.ds(pl.multiple_of(c * rows_per_pass, rows_per_pass), rows_per_pass)
        l0, l1, l2 = scr[1][rows, :], scr[3][rows, :], scr[5][rows, :]
        mx = jnp.maximum(jnp.maximum(l0, l1), l2)
        w0, w1, w2 = jnp.exp(l0 - mx), jnp.exp(l1 - mx), jnp.exp(l2 - mx)
        num = w0 * scr[0][rows, :] + w1 * scr[2][rows, :] + w2 * scr[4][rows, :]
        o_ref[rows, :] = (num / (w0 + w1 + w2)).astype(o_ref.dtype)
        return carry

    lax.fori_loop(0, ATT_TILE // rows_per_pass, merge, 0)


def _attn_prompt(z, slopes):
    t = z.shape[1]
    assert t % ATT_TILE == 0
    in_specs = [pl.BlockSpec(memory_space=pltpu.SMEM)]
    args = [slopes]
    for g, (win, dil) in enumerate(PATTERNS):
        assert win // dil == NKEYS
        prev_rows = NKEYS * dil
        ratio = ATT_TILE // prev_rows
        own = lambda base, g=g: pl.BlockSpec((1, ATT_TILE, LANES), lambda h, n: (base + g * HPG + h, n, 0))
        prev = lambda base, g=g, ratio=ratio, prev_rows=prev_rows: pl.BlockSpec(
            (1, prev_rows, LANES), lambda h, n: (base + g * HPG + h, jnp.maximum(n * ratio - 1, 0), 0))
        in_specs += [own(SLOT_Q), own(SLOT_K), prev(SLOT_K), own(SLOT_V), prev(SLOT_V)]
        args += [z] * 5
    return pl.pallas_call(
        _attn_prompt_kernel,
        grid=(HPG, t // ATT_TILE),
        in_specs=in_specs,
        out_specs=pl.BlockSpec((ATT_TILE, LANES), lambda h, n: (n, h)),
        out_shape=jax.ShapeDtypeStruct((t, HPG * LANES), BF16),
        scratch_shapes=[pltpu.VMEM((ATT_TILE, LANES), F32)] * 6 + [pltpu.VMEM((9, NKEYS, NKEYS), F32)],
        compiler_params=_params("parallel", "arbitrary"),
        name="attn_prompt",
    )(*args)


def _attn_sample_kernel(slopes_ref, z_ref, c0_ref, c1_ref, c2_ref, o_ref, n0_ref, n1_ref, n2_ref, *scr, n_new):
    caches = (c0_ref, c1_ref, c2_ref)
    new_caches = (n0_ref, n1_ref, n2_ref)
    bufs, sem, stage, out_sem = scr[:3], scr[3], scr[4:7], scr[7]
    b = pl.program_id(0)
    s = n_new
    scale = float(LANES) ** -0.5

    def copies():
        out = []
        for g in range(N_GROUPS):
            for kv in range(2):
                for h in range(HPG):
                    out.append(pltpu.make_async_copy(caches[g].at[0, b, :, kv, h, :], bufs[g].at[kv, h],
                                                     sem.at[g, kv, h]))
        return out

    def window_updates():
        out = []
        for g in range(N_GROUPS):
            keep = bufs[g].shape[2] - s
            out.append(pltpu.make_async_copy(caches[g].at[0, b, pl.ds(s, keep)],
                                             new_caches[g].at[0, b, pl.ds(0, keep)], out_sem.at[0, g]))
            out.append(pltpu.make_async_copy(stage[g], new_caches[g].at[0, b, pl.ds(keep, s)], out_sem.at[1, g]))
        return out

    for g in range(N_GROUPS):
        for kv in range(2):
            for h in range(HPG):
                stage[g][:, kv, h, :] = z_ref[12 * (kv + 1) + g * HPG + h, 0]
    for cp in copies() + window_updates():
        cp.start()
    for cp in copies():
        cp.wait()

    for h in range(HPG):
        outs, lses = [], []
        for g, (win, dil) in enumerate(PATTERNS):
            length = bufs[g].shape[2]
            q = z_ref[g * HPG + h, 0]
            kn = z_ref[12 + g * HPG + h, 0]
            vn = z_ref[24 + g * HPG + h, 0]
            slope = slopes_ref[g, h]
            qb = q.astype(BF16)
            qi = lax.broadcasted_iota(jnp.int32, (s, length), 0)
            ri = lax.broadcasted_iota(jnp.int32, (s, length), 1)
            dist = length + qi - ri
            ok = (dist <= win) & ((dist & (dil - 1)) == 0)
            sc = _dot_nt(qb, bufs[g][0, h].astype(BF16)) * scale - slope * dist.astype(F32)
            sc = jnp.where(ok, sc, NEG)
            qn = lax.broadcasted_iota(jnp.int32, (s, s), 0)
            rn = lax.broadcasted_iota(jnp.int32, (s, s), 1)
            dn = qn - rn
            okn = (dn >= 0) & ((dn & (dil - 1)) == 0)
            sn = _dot_nt(qb, kn.astype(BF16)) * scale - slope * dn.astype(F32)
            sn = jnp.where(okn, sn, NEG)
            m = jnp.maximum(jnp.max(sc, axis=-1, keepdims=True), jnp.max(sn, axis=-1, keepdims=True))
            pc = jnp.exp(sc - m)
            pn = jnp.exp(sn - m)
            den = jnp.sum(pc, axis=-1, keepdims=True) + jnp.sum(pn, axis=-1, keepdims=True)
            acc = _dot(pc.astype(BF16), bufs[g][1, h].astype(BF16)) + _dot(pn.astype(BF16), vn.astype(BF16))
            outs.append(acc / den)
            lses.append(m + jnp.log(den))
        mx = jnp.maximum(jnp.maximum(lses[0], lses[1]), lses[2])
        ws = [jnp.exp(l - mx) for l in lses]
        num = ws[0] * outs[0] + ws[1] * outs[1] + ws[2] * outs[2]
        o_ref[0, :, h * LANES:(h + 1) * LANES] = num / (ws[0] + ws[1] + ws[2])

    for cp in window_updates():
        cp.wait()


def _attn_sample(z_qkv, caches, slopes, n_seq, n_new):
    z4 = z_qkv.reshape(3 * N_GROUPS * HPG, n_seq, n_new, LANES)
    for g, (win, _) in enumerate(PATTERNS):
        assert caches[g].shape[2] == win, "window buffers shorter than the window are not supported"
    bufs = [pltpu.VMEM((2, HPG, c.shape[2], LANES), F32) for c in caches]
    return pl.pallas_call(
        functools.partial(_attn_sample_kernel, n_new=n_new),
        grid=(n_seq,),
        in_specs=[pl.BlockSpec(memory_space=pltpu.SMEM),
                  pl.BlockSpec((z4.shape[0], 1, n_new, LANES), lambda b: (0, b, 0, 0)),
                  pl.BlockSpec(memory_space=pl.ANY), pl.BlockSpec(memory_space=pl.ANY),
                  pl.BlockSpec(memory_space=pl.ANY)],
        out_specs=[pl.BlockSpec((1, n_new, HPG * LANES), lambda b: (b, 0, 0))]
        + [pl.BlockSpec(memory_space=pl.ANY)] * N_GROUPS,
        out_shape=[jax.ShapeDtypeStruct((n_seq, n_new, HPG * LANES), F32)]
        + [jax.ShapeDtypeStruct(c.shape, c.dtype) for c in caches],
        scratch_shapes=bufs + [pltpu.SemaphoreType.DMA((N_GROUPS, 2, HPG))]
        + [pltpu.VMEM((n_new, 2, HPG, LANES), F32)] * N_GROUPS + [pltpu.SemaphoreType.DMA((2, N_GROUPS))],
        compiler_params=_params("arbitrary"),
        name="attn_sample",
    )(slopes, z4, *caches)


def _mix_kernel(oa_ref, ob_ref, ga_ref, gb_ref, x_ref, wa_ref, wb_ref, wo_ref, o_ref):
    oa = jnp.concatenate([oa_ref[h] for h in range(oa_ref.shape[0])], axis=1)
    ya = _dot(oa, wa_ref[...])
    yb = _dot(ob_ref[...], wb_ref[...])
    parts = []
    for c in range(ga_ref.shape[0]):
        cols = slice(c * LANES, (c + 1) * LANES)
        mix = _sigmoid(ga_ref[c].astype(F32)) * ya[:, cols] + _sigmoid(gb_ref[c].astype(F32)) * yb[:, cols]
        parts.append(mix.astype(BF16))
    o_ref[...] = x_ref[...] + _dot(jnp.concatenate(parts, axis=1), wo_ref[...])


def _mix(o_a, o_b, z_gate, x, w_a, w_b, w_o, tm):
    m, d = x.shape
    ng = d // LANES
    assert m % tm == 0
    full = lambda a: pl.BlockSpec(a.shape, lambda i: (0, 0))
    return pl.pallas_call(
        _mix_kernel,
        grid=(m // tm,),
        in_specs=[pl.BlockSpec((N_HA, tm, LANES), lambda i: (0, i, 0)),
                  pl.BlockSpec((tm, o_b.shape[1]), lambda i: (i, 0)),
                  pl.BlockSpec((ng, tm, LANES), lambda i: (SLOT_GATE // ng, i, 0)),
                  pl.BlockSpec((ng, tm, LANES), lambda i: (SLOT_GATE // ng + 1, i, 0)),
                  pl.BlockSpec((tm, d), lambda i: (i, 0)),
                  full(w_a), full(w_b), full(w_o)],
        out_specs=pl.BlockSpec((tm, d), lambda i: (i, 0)),
        out_shape=jax.ShapeDtypeStruct((m, d), F32),
        compiler_params=_params("parallel"),
        name="mix",
    )(o_a, o_b, z_gate, z_gate, x, w_a, w_b, w_o)


def _ffn_kernel(*refs, tm, tf, seq_rows):
    (x_ref, nw_ref, wug_ref, wua_ref, cwg_ref, cwa_ref, cbg_ref, cba_ref, wd_ref, fw_ref) = refs[:10]
    if seq_rows is None:
        y_ref, sg_ref, sa_ref, h_ref, acc_ref, ug_ref, ua_ref, carry_ref = refs[10:]
    else:
        pg_ref, pa_ref, y_ref, sg_ref, sa_ref, h_ref, acc_ref = refs[10:]
    i = pl.program_id(0)
    j = pl.program_id(1)

    @pl.when(j == 0)
    def _():
        x = x_ref[...]
        ms = jnp.mean(x * x, axis=-1, keepdims=True)
        h_ref[...] = (x * lax.rsqrt(ms + EPS) * nw_ref[...]).astype(BF16)
        acc_ref[...] = jnp.zeros_like(acc_ref)

    if seq_rows is None:
        @pl.when((i == 0) & (j == 0))
        def _():
            carry_ref[...] = jnp.zeros_like(carry_ref)

    def conv(part, wu_ref, cw_ref, cb_ref):
        u = _dot(h_ref[...], wu_ref[...])
        if seq_rows is None:
            ub_ref, s_ref = ((ug_ref, sg_ref), (ua_ref, sa_ref))[part]
            cols = pl.ds(pl.multiple_of(j * tf, tf), tf)
            ub_ref[pl.ds(0, 8), :] = carry_ref[part, :, cols]
            ub_ref[pl.ds(8, tm), :] = u
            u1 = ub_ref[pl.ds(7, tm), :]
            u2 = ub_ref[pl.ds(6, tm), :]
            carry_ref[part, :, cols] = ub_ref[pl.ds(tm, 8), :]
            s_ref[0] = ub_ref[pl.ds(tm + 6, 2), :]
        else:
            p_ref, s_ref = ((pg_ref, sg_ref), (pa_ref, sa_ref))[part]
            pos = lax.broadcasted_iota(jnp.int32, (tm, 1), 0) % seq_rows
            u1 = jnp.where(pos == 0, p_ref[1], pltpu.roll(u, 1, 0))
            u2 = jnp.where(pos == 0, p_ref[0], jnp.where(pos == 1, p_ref[1], pltpu.roll(u, 2, 0)))
            s_ref[...] = u
        return cb_ref[...] + cw_ref[2:3, :] * u + cw_ref[1:2, :] * u1 + cw_ref[0:1, :] * u2

    cg = conv(0, wug_ref, cwg_ref, cbg_ref)
    ca = conv(1, wua_ref, cwa_ref, cba_ref)
    act = (cg * _sigmoid(cg) * ca).astype(BF16)
    acc_ref[...] += _dot(act, wd_ref[...])

    @pl.when(j == pl.num_programs(1) - 1)
    def _():
        x2 = x_ref[...] + acc_ref[...]
        ms = jnp.mean(x2 * x2, axis=-1, keepdims=True)
        y_ref[...] = x2 * lax.rsqrt(ms + EPS) * fw_ref[...]


def _ffn(x1, nw, w_up, conv_w, conv_b, w_down, fw, *, tm, tf, seq_rows=None, prev=None):
    m, d = x1.shape
    dff = w_down.shape[0]
    assert m % tm == 0 and dff % tf == 0
    nf = dff // tf
    row = lambda i, j: (i, 0)
    const = lambda i, j: (0, 0)
    gcol = lambda i, j: (0, j)
    acol = lambda i, j: (0, nf + j)
    in_specs = [pl.BlockSpec((tm, d), row), pl.BlockSpec((1, d), const),
                pl.BlockSpec((d, tf), gcol), pl.BlockSpec((d, tf), acol),
                pl.BlockSpec((3, tf), gcol), pl.BlockSpec((3, tf), acol),
                pl.BlockSpec((1, tf), gcol), pl.BlockSpec((1, tf), acol),
                pl.BlockSpec((tf, d), lambda i, j: (j, 0)), pl.BlockSpec((1, d), const)]
    args = [x1, nw, w_up, w_up, conv_w, conv_w, conv_b, conv_b, w_down, fw]
    scratch = [pltpu.VMEM((tm, d), BF16), pltpu.VMEM((tm, d), F32)]
    if seq_rows is None:
        state = jax.ShapeDtypeStruct((m // tm, 2, dff), F32)
        state_spec = pl.BlockSpec((1, 2, tf), lambda i, j: (i, 0, j))
        scratch += [pltpu.VMEM((tm + 8, tf), F32), pltpu.VMEM((tm + 8, tf), F32),
                    pltpu.VMEM((2, 8, dff), F32)]
    else:
        assert m // tm == 1 and tm % seq_rows == 0
        in_specs += [pl.BlockSpec((2, tm, tf), lambda i, j: (0, 0, j)),
                     pl.BlockSpec((2, tm, tf), lambda i, j: (0, 0, nf + j))]
        args += [prev, prev]
        state = jax.ShapeDtypeStruct((m, dff), F32)
        state_spec = pl.BlockSpec((tm, tf), lambda i, j: (i, j))
    return pl.pallas_call(
        functools.partial(_ffn_kernel, tm=tm, tf=tf, seq_rows=seq_rows),
        grid=(m // tm, nf),
        in_specs=in_specs,
        out_specs=[pl.BlockSpec((tm, d), row), state_spec, state_spec],
        out_shape=[jax.ShapeDtypeStruct((m, d), F32), state, state],
        scratch_shapes=scratch,
        compiler_params=_params("arbitrary", "arbitrary"),
        name="ffn",
    )(*args)


def _alibi_slopes():
    n = N_GROUPS * HPG
    s = 2.0 ** (-8.0 * np.arange(1, n + 1, dtype=np.float32) / n)
    return jnp.asarray(s, F32).reshape(N_GROUPS, HPG)


def _kv_rows(z, g, rows):
    k = z[SLOT_K + g * HPG:SLOT_K + (g + 1) * HPG, rows]
    v = z[SLOT_V + g * HPG:SLOT_V + (g + 1) * HPG, rows]
    return jnp.stack([k, v], axis=0).transpose(2, 0, 1, 3)


def kernel(x_prompt, x_sample, state_hgrn, cache_kv_g0, cache_kv_g1, cache_kv_g2, state_ffn_conv,
           norm_mix_w, w_in, lb_logits, hgrn_norm_w, w_proj_a, w_proj_b, w_out, norm_ffn_w,
           w_up, ffn_conv_w, ffn_conv_b, w_down, norm_final_w):
    assert w_in.shape[0] == 1, "single-layer trunk"
    _, t, d = x_prompt.shape
    n_seq, n_new, _ = x_sample.shape
    dff = w_down.shape[1]
    slopes = _alibi_slopes()
    lb = jnp.cumsum(jax.nn.softmax(lb_logits.astype(F32), axis=0), axis=0)[0].reshape(N_HA, 1, LANES)
    hnw = hgrn_norm_w[0].reshape(N_HA, 1, LANES)
    nmw = norm_mix_w[0].reshape(1, d)
    nfw = norm_ffn_w[0].reshape(1, d)
    fw = norm_final_w.reshape(1, d)
    n_a, n_qkv = 4 * N_HA * LANES, 3 * N_GROUPS * HPG * LANES
    w_in_b = jnp.concatenate([w_in[0, :, :n_a], w_in[0, :, n_a + n_qkv:], w_in[0, :, n_a:n_a + n_qkv]],
                             axis=1).astype(BF16)
    w_a_b = w_proj_a[0].astype(BF16)
    w_b_b = w_proj_b[0].astype(BF16)
    w_o_b = w_out[0].astype(BF16)
    w_up_b = w_up[0].astype(BF16)
    w_dn_b = w_down[0].astype(BF16)
    cw = ffn_conv_w[0]
    cb = ffn_conv_b[0].reshape(1, 2 * dff)
    caches = (cache_kv_g0, cache_kv_g1, cache_kv_g2)

    xp = x_prompt[0]
    z = _inproj(xp, nmw, w_in_b, 1024, 1280)
    o_a, s_p = _hgrn(z, jnp.zeros((1, N_HA, LANES, LANES), F32), lb, hnw,
                     n_seq=1, rows_per_seq=t, chunk=64, valid=64, rows_per_step=512)
    o_b = _attn_prompt(z, slopes)
    x1 = _mix(o_a, o_b, z, xp, w_a_b, w_b_b, w_o_b, 256)
    y_p, sg, sa = _ffn(x1, nfw, w_up_b, cw, cb, w_dn_b, fw, tm=512, tf=512)
    conv_p = jnp.concatenate([sg[-1], sa[-1]], axis=1)[None, None]
    kv_p = [_kv_rows(z, g, slice(t - min(win, t), t))[None, None] for g, (win, _) in enumerate(PATTERNS)]

    ms = n_seq * n_new
    xs = x_sample.reshape(ms, d)
    z = _inproj(xs, nmw, w_in_b, ms, 1280)
    pad = 8
    z_a_pad = jnp.pad(z[:SLOT_GATE].reshape(SLOT_GATE, n_seq, n_new, LANES),
                      ((0, 0), (0, 0), (0, pad - n_new), (0, 0)))
    o_a, s_s = _hgrn(z_a_pad.reshape(SLOT_GATE, n_seq * pad, LANES), state_hgrn[0], lb, hnw,
                     n_seq=n_seq, rows_per_seq=pad, chunk=pad, valid=n_new, rows_per_step=pad)
    o_a = o_a.reshape(N_HA, n_seq, pad, LANES)[:, :, :n_new].reshape(N_HA, ms, LANES)
    o_b, *kv_s = _attn_sample(z[SLOT_Q:], caches, slopes, n_seq, n_new)
    o_b = o_b.reshape(ms, HPG * LANES).astype(BF16)
    x1 = _mix(o_a, o_b, z, xs, w_a_b, w_b_b, w_o_b, ms)
    prev = jnp.repeat(state_ffn_conv[0].transpose(1, 0, 2), n_new, axis=1)
    y_s, ug, ua = _ffn(x1, nfw, w_up_b, cw, cb, w_dn_b, fw, tm=ms, tf=512, seq_rows=n_new, prev=prev)
    u = jnp.concatenate([ug, ua], axis=1).reshape(n_seq, n_new, 2 * dff)
    conv_s = u[:, n_new - 2:][None]
    return (y_p[None], y_s.reshape(n_seq, n_new, d), s_p[None], s_s[None],
            kv_p[0], kv_s[0], kv_p[1], kv_s[1], kv_p[2], kv_s[2], conv_p, conv_s)
```

```python
import functools

import numpy as np
import jax
import jax.numpy as jnp
from jax import lax
from jax.experimental import pallas as pl
from jax.experimental.pallas import tpu as pltpu

F32 = jnp.float32
BF16 = jnp.bfloat16

LANES = 128
EPS = 1e-6
NEG = -1e30
VMEM_LIMIT = 56 * 1024 * 1024

N_HA = 8
N_GROUPS = 3
HPG = 4
PATTERNS = ((128, 1), (512, 4), (2048, 16))
NKEYS = 128
SLOT_A = 0
SLOT_GATE = 32
SLOT_Q = 64
SLOT_K = 76
SLOT_V = 88


def _dot(a, b):
    return jnp.dot(a, b, preferred_element_type=F32)


def _dot_nt(a, b):
    return lax.dot_general(a, b, (((1,), (1,)), ((), ())), preferred_element_type=F32)


def _dot_tn(a, b):
    return lax.dot_general(a, b, (((0,), (0,)), ((), ())), preferred_element_type=F32)


def _sigmoid(x):
    return 1.0 / (1.0 + jnp.exp(-x))


def _params(*sem):
    return pltpu.CompilerParams(dimension_semantics=sem, vmem_limit_bytes=VMEM_LIMIT)


def _inproj_kernel(x_ref, nw_ref, w_ref, o_ref, h_ref):
    @pl.when(pl.program_id(1) == 0)
    def _():
        x = x_ref[...]
        ms = jnp.mean(x * x, axis=-1, keepdims=True)
        h_ref[...] = (x * lax.rsqrt(ms + EPS) * nw_ref[...]).astype(BF16)

    z = _dot(h_ref[...], w_ref[...])
    for c in range(o_ref.shape[0]):
        o_ref[c] = z[:, c * LANES:(c + 1) * LANES].astype(o_ref.dtype)


def _inproj(x, nw, w, tm, tn):
    m, d = x.shape
    n_slots = w.shape[1] // LANES
    spt = tn // LANES
    assert m % tm == 0 and n_slots % spt == 0
    return pl.pallas_call(
        _inproj_kernel,
        grid=(m // tm, n_slots // spt),
        in_specs=[
            pl.BlockSpec((tm, d), lambda i, j: (i, 0)),
            pl.BlockSpec((1, d), lambda i, j: (0, 0)),
            pl.BlockSpec((d, tn), lambda i, j: (0, j)),
        ],
        out_specs=pl.BlockSpec((spt, tm, LANES), lambda i, j: (j, i, 0)),
        out_shape=jax.ShapeDtypeStruct((n_slots, m, LANES), F32),
        scratch_shapes=[pltpu.VMEM((tm, d), BF16)],
        compiler_params=_params("parallel", "arbitrary"),
        name="inproj",
    )(x, nw, w)


HGRN_HEAD_BATCH = 8


def _hgrn_levels(c):
    levels, h = [], c // 2
    while h >= 1:
        levels.append(h)
        h //= 2
    return levels


def _hgrn_matrix(c):
    r = np.arange(c)
    low = (r[None, :] <= r[:, None]).astype(np.float32)
    mats = [low, 1.0 - low]
    for h in _hgrn_levels(c)[:-1]:
        mid = (r // (2 * h)) * (2 * h) + h
        ref = (r[None, :] <= (mid - 1)[:, None]).astype(np.float32)
        mats.append(low - ref)
    return np.concatenate(mats, axis=0)


def _hgrn_kernel(zq_ref, zf_ref, zi_ref, zg_ref, s0_ref, lb_ref, nw_ref, a_ref, o_ref, sout_ref,
                 st_ref, *, chunk, valid, n_chunks):
    c = chunk
    levels = _hgrn_levels(c)
    n = pl.program_id(1)

    @pl.when(n == 0)
    def _():
        for h in range(N_HA):
            st_ref[h] = s0_ref[0, h].T

    row = lax.broadcasted_iota(jnp.int32, (c, 1), 0)
    ti = lax.broadcasted_iota(jnp.int32, (c, c), 0)
    si = lax.broadcasted_iota(jnp.int32, (c, c), 1)
    eye = ti == si
    pair = [((ti // (2 * h)) == (si // (2 * h))) & ((ti % (2 * h)) >= h) & ((si % (2 * h)) < h) for h in levels]
    a_mat = a_ref[...]

    n_hb = N_HA // HGRN_HEAD_BATCH

    def one_chunk(it, carry):
        r0 = pl.multiple_of((it // n_hb) * c, c)
        if n_hb == 1:
            heads = list(range(N_HA))
        else:
            heads = [(it % n_hb) * HGRN_HEAD_BATCH + hh for hh in range(HGRN_HEAD_BATCH)]

        def gates(h):
            zq = zq_ref[h, pl.ds(r0, c), :].astype(F32)
            zf = zf_ref[h, pl.ds(r0, c), :].astype(F32)
            v = zi_ref[h, pl.ds(r0, c), :].astype(F32)
            zg = zg_ref[h, pl.ds(r0, c), :].astype(F32)
            lb = lb_ref[h]
            f = lb + (1.0 - lb) * _sigmoid(zf)
            logf = jnp.log(f)
            k = 1.0 - f
            if valid < c:
                live = row < valid
                f = jnp.where(live, f, 1.0)
                logf = jnp.where(live, logf, 0.0)
                k = jnp.where(live, k, 0.0)
            q = zq * _sigmoid(zq)
            hi = logf.astype(BF16)
            lo = (logf - hi.astype(F32)).astype(BF16)
            rs = _dot(a_mat, jnp.concatenate([hi, lo], axis=1))
            return q, k, f, v, zg, rs[:, :LANES] + rs[:, LANES:]

        def mix_tokens(h, q, k, f, v, zg, rs):
            cum = rs[0:c]
            rev = rs[c:2 * c]
            total = cum[c - 1:c]
            st = st_ref[h]
            vb = v.astype(BF16)
            inter = _dot_nt((q * jnp.exp(cum)).astype(BF16), st.astype(BF16))
            scores = jnp.where(eye, _dot_nt(q.astype(BF16), k.astype(BF16)), 0.0)
            for li in range(len(levels)):
                if levels[li] == 1:
                    qe, ke = q * f, k
                else:
                    e = jnp.exp(-jnp.abs(rs[(2 + li) * c:(3 + li) * c]))
                    qe, ke = q * e, k * e
                scores = scores + jnp.where(pair[li], _dot_nt(qe.astype(BF16), ke.astype(BF16)), 0.0)
            o = inter + _dot(scores.astype(BF16), vb)
            kdec = (k * jnp.exp(rev)).astype(BF16)
            st_ref[h] = st * jnp.exp(total) + _dot_tn(vb, kdec)
            ms = jnp.mean(o * o, axis=-1, keepdims=True)
            on = o * lax.rsqrt(ms + EPS) * nw_ref[h] * (zg * _sigmoid(zg))
            o_ref[h, pl.ds(r0, c), :] = on.astype(o_ref.dtype)

        staged = [gates(h) for h in heads]
        for h, args in zip(heads, staged):
            mix_tokens(h, *args)
        return carry

    lax.fori_loop(0, n_chunks * n_hb, one_chunk, 0)

    @pl.when(n == pl.num_programs(1) - 1)
    def _():
        for h in range(N_HA):
            sout_ref[0, h] = st_ref[h].T


def _hgrn(z_a, s0, lb, nw, *, n_seq, rows_per_seq, chunk, valid, rows_per_step):
    m = z_a.shape[1]
    assert m == n_seq * rows_per_seq and rows_per_seq % rows_per_step == 0 and rows_per_step % chunk == 0
    nb = rows_per_seq // rows_per_step
    a_mat = jnp.asarray(_hgrn_matrix(chunk), BF16)
    zspec = lambda s: pl.BlockSpec((N_HA, rows_per_step, LANES), lambda b, n: (s, b * nb + n, 0))
    sspec = pl.BlockSpec((1, N_HA, LANES, LANES), lambda b, n: (b, 0, 0, 0))
    vspec = pl.BlockSpec((N_HA, 1, LANES), lambda b, n: (0, 0, 0))
    return pl.pallas_call(
        functools.partial(_hgrn_kernel, chunk=chunk, valid=valid, n_chunks=rows_per_step // chunk),
        grid=(n_seq, nb),
        in_specs=[zspec(0), zspec(1), zspec(2), zspec(3), sspec, vspec, vspec,
                  pl.BlockSpec(a_mat.shape, lambda b, n: (0, 0))],
        out_specs=[pl.BlockSpec((N_HA, rows_per_step, LANES), lambda b, n: (0, b * nb + n, 0)), sspec],
        out_shape=[jax.ShapeDtypeStruct((N_HA, m, LANES), BF16),
                   jax.ShapeDtypeStruct((n_seq, N_HA, LANES, LANES), F32)],
        scratch_shapes=[pltpu.VMEM((N_HA, LANES, LANES), F32)],
        compiler_params=_params("parallel", "arbitrary"),
        name="hgrn",
    )(z_a, z_a, z_a, z_a, s0, lb, nw, a_mat)


ATT_TILE = 2048
ATT_BATCH = 4


def _band_softmax(q, kp, ko, vp, vo, bias_p, bias_o, scale):
    qb = (q * scale).astype(BF16)
    ones = jnp.ones(vp.shape, BF16)
    sp = _dot_nt(qb, kp.astype(BF16)) + bias_p
    so = _dot_nt(qb, ko.astype(BF16)) + bias_o
    m = jnp.maximum(jnp.max(sp, axis=-1, keepdims=True), jnp.max(so, axis=-1, keepdims=True))
    pp = jnp.exp(sp - m).astype(BF16)
    po = jnp.exp(so - m).astype(BF16)
    acc = (_dot(pp, jnp.concatenate([vp.astype(BF16), ones], axis=1))
           + _dot(po, jnp.concatenate([vo.astype(BF16), ones], axis=1)))
    den = acc[:, LANES:]
    return acc[:, :LANES] / den, m + jnp.log(den)


def _attn_prompt_kernel(slopes_ref, *refs):
    ins, o_ref, scr, bias_ref = refs[:15], refs[15], refs[16:22], refs[22]
    h = pl.program_id(0)
    n = pl.program_id(1)
    blk = NKEYS
    scale = float(LANES) ** -0.5
    ti = lax.broadcasted_iota(jnp.int32, (blk, blk), 0)
    si = lax.broadcasted_iota(jnp.int32, (blk, blk), 1)
    dist_o = (ti - si).astype(F32)
    dist_p = dist_o + float(blk)
    for g, (_, dil) in enumerate(PATTERNS):
        q_ref, ko_ref, kp_ref, vo_ref, vp_ref = ins[5 * g:5 * g + 5]
        og_ref, lg_ref = scr[2 * g], scr[2 * g + 1]
        slope_d = slopes_ref[g, h] * float(dil)
        span = blk * dil
        nj = ATT_TILE // span
        bias_prev = jnp.where(dist_p <= float(blk), -slope_d * dist_p, NEG)
        bias_ref[3 * g] = jnp.where(dist_o >= 0.0, -slope_d * dist_o, NEG)
        bias_ref[3 * g + 1] = bias_prev
        bias_ref[3 * g + 2] = jnp.where(n > 0, bias_prev, NEG)

        def unit(start, prev_start, from_prev_tile, g=g, dil=dil, refs5=(q_ref, ko_ref, kp_ref, vo_ref, vp_ref),
                 outs=(og_ref, lg_ref)):
            q_ref, ko_ref, kp_ref, vo_ref, vp_ref = refs5
            rows = pl.ds(start, blk, stride=dil) if dil > 1 else pl.ds(start, blk)
            prows = pl.ds(prev_start, blk, stride=dil) if dil > 1 else pl.ds(prev_start, blk)
            kp = kp_ref[0, prows, :] if from_prev_tile else ko_ref[0, prows, :]
            vp = vp_ref[0, prows, :] if from_prev_tile else vo_ref[0, prows, :]
            bias_p = bias_ref[3 * g + 2] if from_prev_tile else bias_ref[3 * g + 1]
            o, lse = _band_softmax(q_ref[0, rows, :], kp, ko_ref[0, rows, :], vp, vo_ref[0, rows, :],
                                   bias_p, bias_ref[3 * g], scale)
            outs[0][rows, :] = o
            outs[1][rows, :] = lse

        n_batches = nj * dil // ATT_BATCH

        def batch(b, first_tile_block):
            for i in range(ATT_BATCH):
                u = b * ATT_BATCH + i
                j, r = u // dil, u % dil
                if first_tile_block:
                    unit(r, r, True)
                else:
                    unit(j * span + r, (j - 1) * span + r, False)

        b0 = 0
        if dil < ATT_BATCH:
            for u in range(ATT_BATCH):
                j, r = u // dil, u % dil
                if j == 0:
                    unit(r, r, True)
                else:
                    unit(j * span + r, (j - 1) * span + r, False)
            b0 = 1
        n_first = max(dil // ATT_BATCH, b0)

        def first_batches(b, carry):
            batch(b, True)
            return carry

        def later_batches(b, carry):
            batch(b, False)
            return carry

        if n_first > b0:
            lax.fori_loop(b0, n_first, first_batches, 0)
        if n_batches > n_first:
            lax.fori_loop(n_first, n_batches, later_batches, 0)

    rows_per_pass = 256

    def merge(c, carry):
        rows = pl.ds(pl.multiple_of(c * rows_per_pass, rows_per_pass), rows_per_pass)
        l0, l1, l2 = scr[1][rows, :], scr[3][rows, :], scr[5][rows, :]
        mx = jnp.maximum(jnp.maximum(l0, l1), l2)
        w0, w1, w2 = jnp.exp(l0 - mx), jnp.exp(l1 - mx), jnp.exp(l2 - mx)
        num = w0 * scr[0][rows, :] + w1 * scr[2][rows, :] + w2 * scr[4][rows, :]
        o_ref[rows, :] = (num / (w0 + w1 + w2)).astype(o_ref.dtype)
        return carry

    lax.fori_loop(0, ATT_TILE // rows_per_pass, merge, 0)


def _attn_prompt(z, slopes):
    t = z.shape[1]
    assert t % ATT_TILE == 0
    in_specs = [pl.BlockSpec(memory_space=pltpu.SMEM)]
    args = [slopes]
    for g, (win, dil) in enumerate(PATTERNS):
        assert win // dil == NKEYS
        prev_rows = NKEYS * dil
        ratio = ATT_TILE // prev_rows
        own = lambda base, g=g: pl.BlockSpec((1, ATT_TILE, LANES), lambda h, n: (base + g * HPG + h, n, 0))
        prev = lambda base, g=g, ratio=ratio, prev_rows=prev_rows: pl.BlockSpec(
            (1, prev_rows, LANES), lambda h, n: (base + g * HPG + h, jnp.maximum(n * ratio - 1, 0), 0))
        in_specs += [own(SLOT_Q), own(SLOT_K), prev(SLOT_K), own(SLOT_V), prev(SLOT_V)]
        args += [z] * 5
    return pl.pallas_call(
        _attn_prompt_kernel,
        grid=(HPG, t // ATT_TILE),
        in_specs=in_specs,
        out_specs=pl.BlockSpec((ATT_TILE, LANES), lambda h, n: (n, h)),
        out_shape=jax.ShapeDtypeStruct((t, HPG * LANES), BF16),
        scratch_shapes=[pltpu.VMEM((ATT_TILE, LANES), F32)] * 6 + [pltpu.VMEM((9, NKEYS, NKEYS), F32)],
        compiler_params=_params("parallel", "arbitrary"),
        name="attn_prompt",
    )(*args)


def _attn_sample_kernel(slopes_ref, z_ref, c0_ref, c1_ref, c2_ref, o_ref, *scr, n_new):
    caches = (c0_ref, c1_ref, c2_ref)
    bufs, sem = scr[:3], scr[3]
    b = pl.program_id(0)
    s = n_new
    scale = float(LANES) ** -0.5

    def copies():
        out = []
        for g in range(N_GROUPS):
            for kv in range(2):
                for h in range(HPG):
                    out.append(pltpu.make_async_copy(caches[g].at[0, b, :, kv, h, :], bufs[g].at[kv, h],
                                                     sem.at[g, kv, h]))
        return out

    for cp in copies():
        cp.start()
    for cp in copies():
        cp.wait()

    for h in range(HPG):
        outs, lses = [], []
        for g, (win, dil) in enumerate(PATTERNS):
            length = bufs[g].shape[2]
            q = z_ref[g * HPG + h, 0]
            kn = z_ref[12 + g * HPG + h, 0]
            vn = z_ref[24 + g * HPG + h, 0]
            slope = slopes_ref[g, h]
            qb = q.astype(BF16)
            qi = lax.broadcasted_iota(jnp.int32, (s, length), 0)
            ri = lax.broadcasted_iota(jnp.int32, (s, length), 1)
            dist = length + qi - ri
            ok = (dist <= win) & ((dist & (dil - 1)) == 0)
            sc = _dot_nt(qb, bufs[g][0, h].astype(BF16)) * scale - slope * dist.astype(F32)
            sc = jnp.where(ok, sc, NEG)
            qn = lax.broadcasted_iota(jnp.int32, (s, s), 0)
            rn = lax.broadcasted_iota(jnp.int32, (s, s), 1)
            dn = qn - rn
            okn = (dn >= 0) & ((dn & (dil - 1)) == 0)
            sn = _dot_nt(qb, kn.astype(BF16)) * scale - slope * dn.astype(F32)
            sn = jnp.where(okn, sn, NEG)
            m = jnp.maximum(jnp.max(sc, axis=-1, keepdims=True), jnp.max(sn, axis=-1, keepdims=True))
            pc = jnp.exp(sc - m)
            pn = jnp.exp(sn - m)
            den = jnp.sum(pc, axis=-1, keepdims=True) + jnp.sum(pn, axis=-1, keepdims=True)
            acc = _dot(pc.astype(BF16), bufs[g][1, h].astype(BF16)) + _dot(pn.astype(BF16), vn.astype(BF16))
            outs.append(acc / den)
            lses.append(m + jnp.log(den))
        mx = jnp.maximum(jnp.maximum(lses[0], lses[1]), lses[2])
        ws = [jnp.exp(l - mx) for l in lses]
        num = ws[0] * outs[0] + ws[1] * outs[1] + ws[2] * outs[2]
        o_ref[0, :, h * LANES:(h + 1) * LANES] = num / (ws[0] + ws[1] + ws[2])


def _attn_sample(z_qkv, caches, slopes, n_seq, n_new):
    z4 = z_qkv.reshape(3 * N_GROUPS * HPG, n_seq, n_new, LANES)
    for g, (win, _) in enumerate(PATTERNS):
        assert caches[g].shape[2] == win, "window buffers shorter than the window are not supported"
    bufs = [pltpu.VMEM((2, HPG, c.shape[2], LANES), F32) for c in caches]
    return pl.pallas_call(
        functools.partial(_attn_sample_kernel, n_new=n_new),
        grid=(n_seq,),
        in_specs=[pl.BlockSpec(memory_space=pltpu.SMEM),
                  pl.BlockSpec((z4.shape[0], 1, n_new, LANES), lambda b: (0, b, 0, 0)),
                  pl.BlockSpec(memory_space=pl.ANY), pl.BlockSpec(memory_space=pl.ANY),
                  pl.BlockSpec(memory_space=pl.ANY)],
        out_specs=pl.BlockSpec((1, n_new, HPG * LANES), lambda b: (b, 0, 0)),
        out_shape=jax.ShapeDtypeStruct((n_seq, n_new, HPG * LANES), F32),
        scratch_shapes=bufs + [pltpu.SemaphoreType.DMA((N_GROUPS, 2, HPG))],
        compiler_params=_params("arbitrary"),
        name="attn_sample",
    )(slopes, z4, *caches)


def _mix_kernel(oa_ref, ob_ref, ga_ref, gb_ref, x_ref, wa_ref, wb_ref, wo_ref, o_ref):
    oa = jnp.concatenate([oa_ref[h] for h in range(oa_ref.shape[0])], axis=1)
    ya = _dot(oa, wa_ref[...])
    yb = _dot(ob_ref[...], wb_ref[...])
    parts = []
    for c in range(ga_ref.shape[0]):
        cols = slice(c * LANES, (c + 1) * LANES)
        mix = _sigmoid(ga_ref[c].astype(F32)) * ya[:, cols] + _sigmoid(gb_ref[c].astype(F32)) * yb[:, cols]
        parts.append(mix.astype(BF16))
    o_ref[...] = x_ref[...] + _dot(jnp.concatenate(parts, axis=1), wo_ref[...])


def _mix(o_a, o_b, z_gate, x, w_a, w_b, w_o, tm):
    m, d = x.shape
    ng = d // LANES
    assert m % tm == 0
    full = lambda a: pl.BlockSpec(a.shape, lambda i: (0, 0))
    return pl.pallas_call(
        _mix_kernel,
        grid=(m // tm,),
        in_specs=[pl.BlockSpec((N_HA, tm, LANES), lambda i: (0, i, 0)),
                  pl.BlockSpec((tm, o_b.shape[1]), lambda i: (i, 0)),
                  pl.BlockSpec((ng, tm, LANES), lambda i: (SLOT_GATE // ng, i, 0)),
                  pl.BlockSpec((ng, tm, LANES), lambda i: (SLOT_GATE // ng + 1, i, 0)),
                  pl.BlockSpec((tm, d), lambda i: (i, 0)),
                  full(w_a), full(w_b), full(w_o)],
        out_specs=pl.BlockSpec((tm, d), lambda i: (i, 0)),
        out_shape=jax.ShapeDtypeStruct((m, d), F32),
        compiler_params=_params("parallel"),
        name="mix",
    )(o_a, o_b, z_gate, z_gate, x, w_a, w_b, w_o)


def _ffn_kernel(*refs, tm, tf, seq_rows):
    (x_ref, nw_ref, wug_ref, wua_ref, cwg_ref, cwa_ref, cbg_ref, cba_ref, wd_ref, fw_ref) = refs[:10]
    if seq_rows is None:
        y_ref, sg_ref, sa_ref, h_ref, acc_ref, ug_ref, ua_ref, carry_ref = refs[10:]
    else:
        pg_ref, pa_ref, y_ref, sg_ref, sa_ref, h_ref, acc_ref = refs[10:]
    i = pl.program_id(0)
    j = pl.program_id(1)

    @pl.when(j == 0)
    def _():
        x = x_ref[...]
        ms = jnp.mean(x * x, axis=-1, keepdims=True)
        h_ref[...] = (x * lax.rsqrt(ms + EPS) * nw_ref[...]).astype(BF16)
        acc_ref[...] = jnp.zeros_like(acc_ref)

    if seq_rows is None:
        @pl.when((i == 0) & (j == 0))
        def _():
            carry_ref[...] = jnp.zeros_like(carry_ref)

    def conv(part, wu_ref, cw_ref, cb_ref):
        u = _dot(h_ref[...], wu_ref[...])
        if seq_rows is None:
            ub_ref, s_ref = ((ug_ref, sg_ref), (ua_ref, sa_ref))[part]
            cols = pl.ds(pl.multiple_of(j * tf, tf), tf)
            ub_ref[pl.ds(0, 8), :] = carry_ref[part, :, cols]
            ub_ref[pl.ds(8, tm), :] = u
            u1 = ub_ref[pl.ds(7, tm), :]
            u2 = ub_ref[pl.ds(6, tm), :]
            carry_ref[part, :, cols] = ub_ref[pl.ds(tm, 8), :]
            s_ref[0] = ub_ref[pl.ds(tm + 6, 2), :]
        else:
            p_ref, s_ref = ((pg_ref, sg_ref), (pa_ref, sa_ref))[part]
            pos = lax.broadcasted_iota(jnp.int32, (tm, 1), 0) % seq_rows
            u1 = jnp.where(pos == 0, p_ref[1], pltpu.roll(u, 1, 0))
            u2 = jnp.where(pos == 0, p_ref[0], jnp.where(pos == 1, p_ref[1], pltpu.roll(u, 2, 0)))
            s_ref[...] = u
        return cb_ref[...] + cw_ref[2:3, :] * u + cw_ref[1:2, :] * u1 + cw_ref[0:1, :] * u2

    cg = conv(0, wug_ref, cwg_ref, cbg_ref)
    ca = conv(1, wua_ref, cwa_ref, cba_ref)
    act = (cg * _sigmoid(cg) * ca).astype(BF16)
    acc_ref[...] += _dot(act, wd_ref[...])

    @pl.when(j == pl.num_programs(1) - 1)
    def _():
        x2 = x_ref[...] + acc_ref[...]
        ms = jnp.mean(x2 * x2, axis=-1, keepdims=True)
        y_ref[...] = x2 * lax.rsqrt(ms + EPS) * fw_ref[...]


def _ffn(x1, nw, w_up, conv_w, conv_b, w_down, fw, *, tm, tf, seq_rows=None, prev=None):
    m, d = x1.shape
    dff = w_down.shape[0]
    assert m % tm == 0 and dff % tf == 0
    nf = dff // tf
    row = lambda i, j: (i, 0)
    const = lambda i, j: (0, 0)
    gcol = lambda i, j: (0, j)
    acol = lambda i, j: (0, nf + j)
    in_specs = [pl.BlockSpec((tm, d), row), pl.BlockSpec((1, d), const),
                pl.BlockSpec((d, tf), gcol), pl.BlockSpec((d, tf), acol),
                pl.BlockSpec((3, tf), gcol), pl.BlockSpec((3, tf), acol),
                pl.BlockSpec((1, tf), gcol), pl.BlockSpec((1, tf), acol),
                pl.BlockSpec((tf, d), lambda i, j: (j, 0)), pl.BlockSpec((1, d), const)]
    args = [x1, nw, w_up, w_up, conv_w, conv_w, conv_b, conv_b, w_down, fw]
    scratch = [pltpu.VMEM((tm, d), BF16), pltpu.VMEM((tm, d), F32)]
    if seq_rows is None:
        state = jax.ShapeDtypeStruct((m // tm, 2, dff), F32)
        state_spec = pl.BlockSpec((1, 2, tf), lambda i, j: (i, 0, j))
        scratch += [pltpu.VMEM((tm + 8, tf), F32), pltpu.VMEM((tm + 8, tf), F32),
                    pltpu.VMEM((2, 8, dff), F32)]
    else:
        assert m // tm == 1 and tm % seq_rows == 0
        in_specs += [pl.BlockSpec((2, tm, tf), lambda i, j: (0, 0, j)),
                     pl.BlockSpec((2, tm, tf), lambda i, j: (0, 0, nf + j))]
        args += [prev, prev]
        state = jax.ShapeDtypeStruct((m, dff), F32)
        state_spec = pl.BlockSpec((tm, tf), lambda i, j: (i, j))
    return pl.pallas_call(
        functools.partial(_ffn_kernel, tm=tm, tf=tf, seq_rows=seq_rows),
        grid=(m // tm, nf),
        in_specs=in_specs,
        out_specs=[pl.BlockSpec((tm, d), row), state_spec, state_spec],
        out_shape=[jax.ShapeDtypeStruct((m, d), F32), state, state],
        scratch_shapes=scratch,
        compiler_params=_params("arbitrary", "arbitrary"),
        name="ffn",
    )(*args)


def _alibi_slopes():
    n = N_GROUPS * HPG
    s = 2.0 ** (-8.0 * np.arange(1, n + 1, dtype=np.float32) / n)
    return jnp.asarray(s, F32).reshape(N_GROUPS, HPG)


def _kv_rows(z, g, rows):
    k = z[SLOT_K + g * HPG:SLOT_K + (g + 1) * HPG, rows]
    v = z[SLOT_V + g * HPG:SLOT_V + (g + 1) * HPG, rows]
    return jnp.stack([k, v], axis=0).transpose(2, 0, 1, 3)


def kernel(x_prompt, x_sample, state_hgrn, cache_kv_g0, cache_kv_g1, cache_kv_g2, state_ffn_conv,
           norm_mix_w, w_in, lb_logits, hgrn_norm_w, w_proj_a, w_proj_b, w_out, norm_ffn_w,
           w_up, ffn_conv_w, ffn_conv_b, w_down, norm_final_w):
    assert w_in.shape[0] == 1, "single-layer trunk"
    _, t, d = x_prompt.shape
    n_seq, n_new, _ = x_sample.shape
    dff = w_down.shape[1]
    slopes = _alibi_slopes()
    lb = jnp.cumsum(jax.nn.softmax(lb_logits.astype(F32), axis=0), axis=0)[0].reshape(N_HA, 1, LANES)
    hnw = hgrn_norm_w[0].reshape(N_HA, 1, LANES)
    nmw = norm_mix_w[0].reshape(1, d)
    nfw = norm_ffn_w[0].reshape(1, d)
    fw = norm_final_w.reshape(1, d)
    n_a, n_qkv = 4 * N_HA * LANES, 3 * N_GROUPS * HPG * LANES
    w_in_b = jnp.concatenate([w_in[0, :, :n_a], w_in[0, :, n_a + n_qkv:], w_in[0, :, n_a:n_a + n_qkv]],
                             axis=1).astype(BF16)
    w_a_b = w_proj_a[0].astype(BF16)
    w_b_b = w_proj_b[0].astype(BF16)
    w_o_b = w_out[0].astype(BF16)
    w_up_b = w_up[0].astype(BF16)
    w_dn_b = w_down[0].astype(BF16)
    cw = ffn_conv_w[0]
    cb = ffn_conv_b[0].reshape(1, 2 * dff)
    caches = (cache_kv_g0, cache_kv_g1, cache_kv_g2)

    xp = x_prompt[0]
    z = _inproj(xp, nmw, w_in_b, 1024, 1280)
    o_a, s_p = _hgrn(z, jnp.zeros((1, N_HA, LANES, LANES), F32), lb, hnw,
                     n_seq=1, rows_per_seq=t, chunk=64, valid=64, rows_per_step=512)
    o_b = _attn_prompt(z, slopes)
    x1 = _mix(o_a, o_b, z, xp, w_a_b, w_b_b, w_o_b, 256)
    y_p, sg, sa = _ffn(x1, nfw, w_up_b, cw, cb, w_dn_b, fw, tm=512, tf=512)
    conv_p = jnp.concatenate([sg[-1], sa[-1]], axis=1)[None, None]
    kv_p = [_kv_rows(z, g, slice(t - min(win, t), t))[None, None] for g, (win, _) in enumerate(PATTERNS)]

    ms = n_seq * n_new
    xs = x_sample.reshape(ms, d)
    z = _inproj(xs, nmw, w_in_b, ms, 1280)
    pad = 8
    z_a_pad = jnp.pad(z[:SLOT_GATE].reshape(SLOT_GATE, n_seq, n_new, LANES),
                      ((0, 0), (0, 0), (0, pad - n_new), (0, 0)))
    o_a, s_s = _hgrn(z_a_pad.reshape(SLOT_GATE, n_seq * pad, LANES), state_hgrn[0], lb, hnw,
                     n_seq=n_seq, rows_per_seq=pad, chunk=pad, valid=n_new, rows_per_step=pad)
    o_a = o_a.reshape(N_HA, n_seq, pad, LANES)[:, :, :n_new].reshape(N_HA, ms, LANES)
    o_b = _attn_sample(z[SLOT_Q:], caches, slopes, n_seq, n_new).reshape(ms, HPG * LANES).astype(BF16)
    x1 = _mix(o_a, o_b, z, xs, w_a_b, w_b_b, w_o_b, ms)
    prev = jnp.repeat(state_ffn_conv[0].transpose(1, 0, 2), n_new, axis=1)
    y_s, ug, ua = _ffn(x1, nfw, w_up_b, cw, cb, w_dn_b, fw, tm=ms, tf=512, seq_rows=n_new, prev=prev)
    u = jnp.concatenate([ug, ua], axis=1).reshape(n_seq, n_new, 2 * dff)
    conv_s = u[:, n_new - 2:][None]
    kv_s = []
    for g in range(N_GROUPS):
        new = _kv_rows(z, g, slice(0, ms)).reshape(n_seq, n_new, 2, HPG, LANES)
        length = caches[g].shape[2]
        kv_s.append(jnp.concatenate([caches[g][0], new], axis=1)[:, -length:][None])

    return (y_p[None], y_s.reshape(n_seq, n_new, d), s_p[None], s_s[None],
            kv_p[0], kv_s[0], kv_p[1], kv_s[1], kv_p[2], kv_s[2], conv_p, conv_s)
```

```python
import functools

import numpy as np
import jax
import jax.numpy as jnp
from jax import lax
from jax.experimental import pallas as pl
from jax.experimental.pallas import tpu as pltpu

F32 = jnp.float32
BF16 = jnp.bfloat16

LANES = 128
EPS = 1e-6
NEG = -1e30
VMEM_LIMIT = 56 * 1024 * 1024

N_HA = 8
N_GROUPS = 3
HPG = 4
PATTERNS = ((128, 1), (512, 4), (2048, 16))
NKEYS = 128
SLOT_A = 0
SLOT_GATE = 32
SLOT_Q = 64
SLOT_K = 76
SLOT_V = 88


def _dot(a, b):
    return jnp.dot(a, b, preferred_element_type=F32)


def _dot_nt(a, b):
    return lax.dot_general(a, b, (((1,), (1,)), ((), ())), preferred_element_type=F32)


def _dot_tn(a, b):
    return lax.dot_general(a, b, (((0,), (0,)), ((), ())), preferred_element_type=F32)


def _sigmoid(x):
    return 1.0 / (1.0 + jnp.exp(-x))


def _params(*sem):
    return pltpu.CompilerParams(dimension_semantics=sem, vmem_limit_bytes=VMEM_LIMIT)


def _inproj_kernel(x_ref, nw_ref, w_ref, o_ref, h_ref):
    @pl.when(pl.program_id(1) == 0)
    def _():
        x = x_ref[...]
        ms = jnp.mean(x * x, axis=-1, keepdims=True)
        h_ref[...] = (x * lax.rsqrt(ms + EPS) * nw_ref[...]).astype(BF16)

    z = _dot(h_ref[...], w_ref[...])
    for c in range(o_ref.shape[0]):
        o_ref[c] = z[:, c * LANES:(c + 1) * LANES].astype(o_ref.dtype)


def _inproj(x, nw, w, tm, tn):
    m, d = x.shape
    n_slots = w.shape[1] // LANES
    spt = tn // LANES
    assert m % tm == 0 and n_slots % spt == 0
    return pl.pallas_call(
        _inproj_kernel,
        grid=(m // tm, n_slots // spt),
        in_specs=[
            pl.BlockSpec((tm, d), lambda i, j: (i, 0)),
            pl.BlockSpec((1, d), lambda i, j: (0, 0)),
            pl.BlockSpec((d, tn), lambda i, j: (0, j)),
        ],
        out_specs=pl.BlockSpec((spt, tm, LANES), lambda i, j: (j, i, 0)),
        out_shape=jax.ShapeDtypeStruct((n_slots, m, LANES), F32),
        scratch_shapes=[pltpu.VMEM((tm, d), BF16)],
        compiler_params=_params("parallel", "arbitrary"),
        name="inproj",
    )(x, nw, w)


HGRN_HEAD_BATCH = 8


def _hgrn_levels(c):
    levels, h = [], c // 2
    while h >= 1:
        levels.append(h)
        h //= 2
    return levels


def _hgrn_matrix(c):
    r = np.arange(c)
    low = (r[None, :] <= r[:, None]).astype(np.float32)
    mats = [low, 1.0 - low]
    for h in _hgrn_levels(c)[:-1]:
        mid = (r // (2 * h)) * (2 * h) + h
        ref = (r[None, :] <= (mid - 1)[:, None]).astype(np.float32)
        mats.append(low - ref)
    return np.concatenate(mats, axis=0)


def _hgrn_kernel(zq_ref, zf_ref, zi_ref, zg_ref, s0_ref, lb_ref, nw_ref, a_ref, o_ref, sout_ref,
                 st_ref, *, chunk, valid, n_chunks):
    c = chunk
    levels = _hgrn_levels(c)
    n = pl.program_id(1)

    @pl.when(n == 0)
    def _():
        for h in range(N_HA):
            st_ref[h] = s0_ref[0, h].T

    row = lax.broadcasted_iota(jnp.int32, (c, 1), 0)
    ti = lax.broadcasted_iota(jnp.int32, (c, c), 0)
    si = lax.broadcasted_iota(jnp.int32, (c, c), 1)
    eye = ti == si
    pair = [((ti // (2 * h)) == (si // (2 * h))) & ((ti % (2 * h)) >= h) & ((si % (2 * h)) < h) for h in levels]
    a_mat = a_ref[...]

    n_hb = N_HA // HGRN_HEAD_BATCH

    def one_chunk(it, carry):
        r0 = pl.multiple_of((it // n_hb) * c, c)
        if n_hb == 1:
            heads = list(range(N_HA))
        else:
            heads = [(it % n_hb) * HGRN_HEAD_BATCH + hh for hh in range(HGRN_HEAD_BATCH)]

        def gates(h):
            zq = zq_ref[h, pl.ds(r0, c), :].astype(F32)
            zf = zf_ref[h, pl.ds(r0, c), :].astype(F32)
            v = zi_ref[h, pl.ds(r0, c), :].astype(F32)
            zg = zg_ref[h, pl.ds(r0, c), :].astype(F32)
            lb = lb_ref[h]
            f = lb + (1.0 - lb) * _sigmoid(zf)
            logf = jnp.log(f)
            k = 1.0 - f
            if valid < c:
                live = row < valid
                f = jnp.where(live, f, 1.0)
                logf = jnp.where(live, logf, 0.0)
                k = jnp.where(live, k, 0.0)
            q = zq * _sigmoid(zq)
            hi = logf.astype(BF16)
            lo = (logf - hi.astype(F32)).astype(BF16)
            rs = _dot(a_mat, jnp.concatenate([hi, lo], axis=1))
            return q, k, f, v, zg, rs[:, :LANES] + rs[:, LANES:]

        def mix_tokens(h, q, k, f, v, zg, rs):
            cum = rs[0:c]
            rev = rs[c:2 * c]
            total = cum[c - 1:c]
            st = st_ref[h]
            vb = v.astype(BF16)
            inter = _dot_nt((q * jnp.exp(cum)).astype(BF16), st.astype(BF16))
            scores = jnp.where(eye, _dot_nt(q.astype(BF16), k.astype(BF16)), 0.0)
            for li in range(len(levels)):
                if levels[li] == 1:
                    qe, ke = q * f, k
                else:
                    e = jnp.exp(-jnp.abs(rs[(2 + li) * c:(3 + li) * c]))
                    qe, ke = q * e, k * e
                scores = scores + jnp.where(pair[li], _dot_nt(qe.astype(BF16), ke.astype(BF16)), 0.0)
            o = inter + _dot(scores.astype(BF16), vb)
            kdec = (k * jnp.exp(rev)).astype(BF16)
            st_ref[h] = st * jnp.exp(total) + _dot_tn(vb, kdec)
            ms = jnp.mean(o * o, axis=-1, keepdims=True)
            on = o * lax.rsqrt(ms + EPS) * nw_ref[h] * (zg * _sigmoid(zg))
            o_ref[h, pl.ds(r0, c), :] = on.astype(o_ref.dtype)

        staged = [gates(h) for h in heads]
        for h, args in zip(heads, staged):
            mix_tokens(h, *args)
        return carry

    lax.fori_loop(0, n_chunks * n_hb, one_chunk, 0)

    @pl.when(n == pl.num_programs(1) - 1)
    def _():
        for h in range(N_HA):
            sout_ref[0, h] = st_ref[h].T


def _hgrn(z_a, s0, lb, nw, *, n_seq, rows_per_seq, chunk, valid, rows_per_step):
    m = z_a.shape[1]
    assert m == n_seq * rows_per_seq and rows_per_seq % rows_per_step == 0 and rows_per_step % chunk == 0
    nb = rows_per_seq // rows_per_step
    a_mat = jnp.asarray(_hgrn_matrix(chunk), BF16)
    zspec = lambda s: pl.BlockSpec((N_HA, rows_per_step, LANES), lambda b, n: (s, b * nb + n, 0))
    sspec = pl.BlockSpec((1, N_HA, LANES, LANES), lambda b, n: (b, 0, 0, 0))
    vspec = pl.BlockSpec((N_HA, 1, LANES), lambda b, n: (0, 0, 0))
    return pl.pallas_call(
        functools.partial(_hgrn_kernel, chunk=chunk, valid=valid, n_chunks=rows_per_step // chunk),
        grid=(n_seq, nb),
        in_specs=[zspec(0), zspec(1), zspec(2), zspec(3), sspec, vspec, vspec,
                  pl.BlockSpec(a_mat.shape, lambda b, n: (0, 0))],
        out_specs=[pl.BlockSpec((N_HA, rows_per_step, LANES), lambda b, n: (0, b * nb + n, 0)), sspec],
        out_shape=[jax.ShapeDtypeStruct((N_HA, m, LANES), BF16),
                   jax.ShapeDtypeStruct((n_seq, N_HA, LANES, LANES), F32)],
        scratch_shapes=[pltpu.VMEM((N_HA, LANES, LANES), F32)],
        compiler_params=_params("parallel", "arbitrary"),
        name="hgrn",
    )(z_a, z_a, z_a, z_a, s0, lb, nw, a_mat)


ATT_TILE = 2048
ATT_BATCH = 8


def _band_softmax(q, kp, ko, vp, vo, bias_p, bias_o, scale):
    qb = (q * scale).astype(BF16)
    ones = jnp.ones(vp.shape, BF16)
    sp = _dot_nt(qb, kp.astype(BF16)) + bias_p
    so = _dot_nt(qb, ko.astype(BF16)) + bias_o
    m = jnp.maximum(jnp.max(sp, axis=-1, keepdims=True), jnp.max(so, axis=-1, keepdims=True))
    pp = jnp.exp(sp - m).astype(BF16)
    po = jnp.exp(so - m).astype(BF16)
    acc = (_dot(pp, jnp.concatenate([vp.astype(BF16), ones], axis=1))
           + _dot(po, jnp.concatenate([vo.astype(BF16), ones], axis=1)))
    den = acc[:, LANES:]
    return acc[:, :LANES] / den, m + jnp.log(den)


def _attn_prompt_kernel(slopes_ref, *refs):
    ins, o_ref, scr, bias_ref = refs[:15], refs[15], refs[16:22], refs[22]
    h = pl.program_id(0)
    n = pl.program_id(1)
    blk = NKEYS
    scale = float(LANES) ** -0.5
    ti = lax.broadcasted_iota(jnp.int32, (blk, blk), 0)
    si = lax.broadcasted_iota(jnp.int32, (blk, blk), 1)
    dist_o = (ti - si).astype(F32)
    dist_p = dist_o + float(blk)
    for g, (_, dil) in enumerate(PATTERNS):
        q_ref, ko_ref, kp_ref, vo_ref, vp_ref = ins[5 * g:5 * g + 5]
        og_ref, lg_ref = scr[2 * g], scr[2 * g + 1]
        slope_d = slopes_ref[g, h] * float(dil)
        span = blk * dil
        nj = ATT_TILE // span
        bias_prev = jnp.where(dist_p <= float(blk), -slope_d * dist_p, NEG)
        bias_ref[3 * g] = jnp.where(dist_o >= 0.0, -slope_d * dist_o, NEG)
        bias_ref[3 * g + 1] = bias_prev
        bias_ref[3 * g + 2] = jnp.where(n > 0, bias_prev, NEG)

        def unit(start, prev_start, from_prev_tile, g=g, dil=dil, refs5=(q_ref, ko_ref, kp_ref, vo_ref, vp_ref),
                 outs=(og_ref, lg_ref)):
            q_ref, ko_ref, kp_ref, vo_ref, vp_ref = refs5
            rows = pl.ds(start, blk, stride=dil) if dil > 1 else pl.ds(start, blk)
            prows = pl.ds(prev_start, blk, stride=dil) if dil > 1 else pl.ds(prev_start, blk)
            kp = kp_ref[0, prows, :] if from_prev_tile else ko_ref[0, prows, :]
            vp = vp_ref[0, prows, :] if from_prev_tile else vo_ref[0, prows, :]
            bias_p = bias_ref[3 * g + 2] if from_prev_tile else bias_ref[3 * g + 1]
            o, lse = _band_softmax(q_ref[0, rows, :], kp, ko_ref[0, rows, :], vp, vo_ref[0, rows, :],
                                   bias_p, bias_ref[3 * g], scale)
            outs[0][rows, :] = o
            outs[1][rows, :] = lse

        n_batches = nj * dil // ATT_BATCH

        def batch(b, first_tile_block):
            for i in range(ATT_BATCH):
                u = b * ATT_BATCH + i
                j, r = u // dil, u % dil
                if first_tile_block:
                    unit(r, r, True)
                else:
                    unit(j * span + r, (j - 1) * span + r, False)

        b0 = 0
        if dil < ATT_BATCH:
            for u in range(ATT_BATCH):
                j, r = u // dil, u % dil
                if j == 0:
                    unit(r, r, True)
                else:
                    unit(j * span + r, (j - 1) * span + r, False)
            b0 = 1
        n_first = max(dil // ATT_BATCH, b0)

        def first_batches(b, carry):
            batch(b, True)
            return carry

        def later_batches(b, carry):
            batch(b, False)
            return carry

        if n_first > b0:
            lax.fori_loop(b0, n_first, first_batches, 0)
        if n_batches > n_first:
            lax.fori_loop(n_first, n_batches, later_batches, 0)

    rows_per_pass = 256

    def merge(c, carry):
        rows = pl.ds(pl.multiple_of(c * rows_per_pass, rows_per_pass), rows_per_pass)
        l0, l1, l2 = scr[1][rows, :], scr[3][rows, :], scr[5][rows, :]
        mx = jnp.maximum(jnp.maximum(l0, l1), l2)
        w0, w1, w2 = jnp.exp(l0 - mx), jnp.exp(l1 - mx), jnp.exp(l2 - mx)
        num = w0 * scr[0][rows, :] + w1 * scr[2][rows, :] + w2 * scr[4][rows, :]
        o_ref[rows, :] = (num / (w0 + w1 + w2)).astype(o_ref.dtype)
        return carry

    lax.fori_loop(0, ATT_TILE // rows_per_pass, merge, 0)


def _attn_prompt(z, slopes):
    t = z.shape[1]
    assert t % ATT_TILE == 0
    in_specs = [pl.BlockSpec(memory_space=pltpu.SMEM)]
    args = [slopes]
    for g, (win, dil) in enumerate(PATTERNS):
        assert win // dil == NKEYS
        prev_rows = NKEYS * dil
        ratio = ATT_TILE // prev_rows
        own = lambda base, g=g: pl.BlockSpec((1, ATT_TILE, LANES), lambda h, n: (base + g * HPG + h, n, 0))
        prev = lambda base, g=g, ratio=ratio, prev_rows=prev_rows: pl.BlockSpec(
            (1, prev_rows, LANES), lambda h, n: (base + g * HPG + h, jnp.maximum(n * ratio - 1, 0), 0))
        in_specs += [own(SLOT_Q), own(SLOT_K), prev(SLOT_K), own(SLOT_V), prev(SLOT_V)]
        args += [z] * 5
    return pl.pallas_call(
        _attn_prompt_kernel,
        grid=(HPG, t // ATT_TILE),
        in_specs=in_specs,
        out_specs=pl.BlockSpec((ATT_TILE, LANES), lambda h, n: (n, h)),
        out_shape=jax.ShapeDtypeStruct((t, HPG * LANES), BF16),
        scratch_shapes=[pltpu.VMEM((ATT_TILE, LANES), F32)] * 6 + [pltpu.VMEM((9, NKEYS, NKEYS), F32)],
        compiler_params=_params("parallel", "arbitrary"),
        name="attn_prompt",
    )(*args)


def _attn_sample_kernel(slopes_ref, z_ref, c0_ref, c1_ref, c2_ref, o_ref, *scr, n_new):
    caches = (c0_ref, c1_ref, c2_ref)
    bufs, sem = scr[:3], scr[3]
    b = pl.program_id(0)
    slot = b % 2
    s = n_new
    scale = float(LANES) ** -0.5

    def copies(seq, sl):
        out = []
        for g in range(N_GROUPS):
            for kv in range(2):
                for h in range(HPG):
                    out.append(pltpu.make_async_copy(caches[g].at[0, seq, :, kv, h, :], bufs[g].at[sl, kv, h],
                                                     sem.at[sl, g, kv, h]))
        return out

    @pl.when(b == 0)
    def _():
        for cp in copies(0, 0):
            cp.start()

    @pl.when(b + 1 < pl.num_programs(0))
    def _():
        for cp in copies(b + 1, 1 - slot):
            cp.start()

    for cp in copies(b, slot):
        cp.wait()

    for h in range(HPG):
        outs, lses = [], []
        for g, (win, dil) in enumerate(PATTERNS):
            length = bufs[g].shape[3]
            q = z_ref[g * HPG + h, 0]
            kn = z_ref[12 + g * HPG + h, 0]
            vn = z_ref[24 + g * HPG + h, 0]
            slope = slopes_ref[g, h]
            qb = q.astype(BF16)
            qi = lax.broadcasted_iota(jnp.int32, (s, length), 0)
            ri = lax.broadcasted_iota(jnp.int32, (s, length), 1)
            dist = length + qi - ri
            ok = (dist <= win) & ((dist & (dil - 1)) == 0)
            sc = _dot_nt(qb, bufs[g][slot, 0, h].astype(BF16)) * scale - slope * dist.astype(F32)
            sc = jnp.where(ok, sc, NEG)
            qn = lax.broadcasted_iota(jnp.int32, (s, s), 0)
            rn = lax.broadcasted_iota(jnp.int32, (s, s), 1)
            dn = qn - rn
            okn = (dn >= 0) & ((dn & (dil - 1)) == 0)
            sn = _dot_nt(qb, kn.astype(BF16)) * scale - slope * dn.astype(F32)
            sn = jnp.where(okn, sn, NEG)
            m = jnp.maximum(jnp.max(sc, axis=-1, keepdims=True), jnp.max(sn, axis=-1, keepdims=True))
            pc = jnp.exp(sc - m)
            pn = jnp.exp(sn - m)
            den = jnp.sum(pc, axis=-1, keepdims=True) + jnp.sum(pn, axis=-1, keepdims=True)
            acc = (_dot(pc.astype(BF16), bufs[g][slot, 1, h].astype(BF16))
                   + _dot(pn.astype(BF16), vn.astype(BF16)))
            outs.append(acc / den)
            lses.append(m + jnp.log(den))
        mx = jnp.maximum(jnp.maximum(lses[0], lses[1]), lses[2])
        ws = [jnp.exp(l - mx) for l in lses]
        num = ws[0] * outs[0] + ws[1] * outs[1] + ws[2] * outs[2]
        o_ref[0, :, h * LANES:(h + 1) * LANES] = num / (ws[0] + ws[1] + ws[2])


def _attn_sample(z_qkv, caches, slopes, n_seq, n_new):
    z4 = z_qkv.reshape(3 * N_GROUPS * HPG, n_seq, n_new, LANES)
    for g, (win, _) in enumerate(PATTERNS):
        assert caches[g].shape[2] == win, "window buffers shorter than the window are not supported"
    bufs = [pltpu.VMEM((2, 2, HPG, c.shape[2], LANES), F32) for c in caches]
    return pl.pallas_call(
        functools.partial(_attn_sample_kernel, n_new=n_new),
        grid=(n_seq,),
        in_specs=[pl.BlockSpec(memory_space=pltpu.SMEM),
                  pl.BlockSpec((z4.shape[0], 1, n_new, LANES), lambda b: (0, b, 0, 0)),
                  pl.BlockSpec(memory_space=pl.ANY), pl.BlockSpec(memory_space=pl.ANY),
                  pl.BlockSpec(memory_space=pl.ANY)],
        out_specs=pl.BlockSpec((1, n_new, HPG * LANES), lambda b: (b, 0, 0)),
        out_shape=jax.ShapeDtypeStruct((n_seq, n_new, HPG * LANES), F32),
        scratch_shapes=bufs + [pltpu.SemaphoreType.DMA((2, N_GROUPS, 2, HPG))],
        compiler_params=_params("arbitrary"),
        name="attn_sample",
    )(slopes, z4, *caches)


def _mix_kernel(oa_ref, ob_ref, ga_ref, gb_ref, x_ref, wa_ref, wb_ref, wo_ref, o_ref):
    oa = jnp.concatenate([oa_ref[h] for h in range(oa_ref.shape[0])], axis=1)
    ya = _dot(oa, wa_ref[...])
    yb = _dot(ob_ref[...], wb_ref[...])
    parts = []
    for c in range(ga_ref.shape[0]):
        cols = slice(c * LANES, (c + 1) * LANES)
        mix = _sigmoid(ga_ref[c].astype(F32)) * ya[:, cols] + _sigmoid(gb_ref[c].astype(F32)) * yb[:, cols]
        parts.append(mix.astype(BF16))
    o_ref[...] = x_ref[...] + _dot(jnp.concatenate(parts, axis=1), wo_ref[...])


def _mix(o_a, o_b, z_gate, x, w_a, w_b, w_o, tm):
    m, d = x.shape
    ng = d // LANES
    assert m % tm == 0
    full = lambda a: pl.BlockSpec(a.shape, lambda i: (0, 0))
    return pl.pallas_call(
        _mix_kernel,
        grid=(m // tm,),
        in_specs=[pl.BlockSpec((N_HA, tm, LANES), lambda i: (0, i, 0)),
                  pl.BlockSpec((tm, o_b.shape[1]), lambda i: (i, 0)),
                  pl.BlockSpec((ng, tm, LANES), lambda i: (SLOT_GATE // ng, i, 0)),
                  pl.BlockSpec((ng, tm, LANES), lambda i: (SLOT_GATE // ng + 1, i, 0)),
                  pl.BlockSpec((tm, d), lambda i: (i, 0)),
                  full(w_a), full(w_b), full(w_o)],
        out_specs=pl.BlockSpec((tm, d), lambda i: (i, 0)),
        out_shape=jax.ShapeDtypeStruct((m, d), F32),
        compiler_params=_params("parallel"),
        name="mix",
    )(o_a, o_b, z_gate, z_gate, x, w_a, w_b, w_o)


def _ffn_kernel(*refs, tm, tf, seq_rows):
    (x_ref, nw_ref, wug_ref, wua_ref, cwg_ref, cwa_ref, cbg_ref, cba_ref, wd_ref, fw_ref) = refs[:10]
    if seq_rows is None:
        y_ref, sg_ref, sa_ref, h_ref, acc_ref, ug_ref, ua_ref, carry_ref = refs[10:]
    else:
        pg_ref, pa_ref, y_ref, sg_ref, sa_ref, h_ref, acc_ref = refs[10:]
    i = pl.program_id(0)
    j = pl.program_id(1)

    @pl.when(j == 0)
    def _():
        x = x_ref[...]
        ms = jnp.mean(x * x, axis=-1, keepdims=True)
        h_ref[...] = (x * lax.rsqrt(ms + EPS) * nw_ref[...]).astype(BF16)
        acc_ref[...] = jnp.zeros_like(acc_ref)

    if seq_rows is None:
        @pl.when((i == 0) & (j == 0))
        def _():
            carry_ref[...] = jnp.zeros_like(carry_ref)

    def conv(part, wu_ref, cw_ref, cb_ref):
        u = _dot(h_ref[...], wu_ref[...])
        if seq_rows is None:
            ub_ref, s_ref = ((ug_ref, sg_ref), (ua_ref, sa_ref))[part]
            cols = pl.ds(pl.multiple_of(j * tf, tf), tf)
            ub_ref[pl.ds(0, 8), :] = carry_ref[part, :, cols]
            ub_ref[pl.ds(8, tm), :] = u
            u1 = ub_ref[pl.ds(7, tm), :]
            u2 = ub_ref[pl.ds(6, tm), :]
            carry_ref[part, :, cols] = ub_ref[pl.ds(tm, 8), :]
            s_ref[0] = ub_ref[pl.ds(tm + 6, 2), :]
        else:
            p_ref, s_ref = ((pg_ref, sg_ref), (pa_ref, sa_ref))[part]
            pos = lax.broadcasted_iota(jnp.int32, (tm, 1), 0) % seq_rows
            u1 = jnp.where(pos == 0, p_ref[1], pltpu.roll(u, 1, 0))
            u2 = jnp.where(pos == 0, p_ref[0], jnp.where(pos == 1, p_ref[1], pltpu.roll(u, 2, 0)))
            s_ref[...] = u
        return cb_ref[...] + cw_ref[2:3, :] * u + cw_ref[1:2, :] * u1 + cw_ref[0:1, :] * u2

    cg = conv(0, wug_ref, cwg_ref, cbg_ref)
    ca = conv(1, wua_ref, cwa_ref, cba_ref)
    act = (cg * _sigmoid(cg) * ca).astype(BF16)
    acc_ref[...] += _dot(act, wd_ref[...])

    @pl.when(j == pl.num_programs(1) - 1)
    def _():
        x2 = x_ref[...] + acc_ref[...]
        ms = jnp.mean(x2 * x2, axis=-1, keepdims=True)
        y_ref[...] = x2 * lax.rsqrt(ms + EPS) * fw_ref[...]


def _ffn(x1, nw, w_up, conv_w, conv_b, w_down, fw, *, tm, tf, seq_rows=None, prev=None):
    m, d = x1.shape
    dff = w_down.shape[0]
    assert m % tm == 0 and dff % tf == 0
    nf = dff // tf
    row = lambda i, j: (i, 0)
    const = lambda i, j: (0, 0)
    gcol = lambda i, j: (0, j)
    acol = lambda i, j: (0, nf + j)
    in_specs = [pl.BlockSpec((tm, d), row), pl.BlockSpec((1, d), const),
                pl.BlockSpec((d, tf), gcol), pl.BlockSpec((d, tf), acol),
                pl.BlockSpec((3, tf), gcol), pl.BlockSpec((3, tf), acol),
                pl.BlockSpec((1, tf), gcol), pl.BlockSpec((1, tf), acol),
                pl.BlockSpec((tf, d), lambda i, j: (j, 0)), pl.BlockSpec((1, d), const)]
    args = [x1, nw, w_up, w_up, conv_w, conv_w, conv_b, conv_b, w_down, fw]
    scratch = [pltpu.VMEM((tm, d), BF16), pltpu.VMEM((tm, d), F32)]
    if seq_rows is None:
        state = jax.ShapeDtypeStruct((m // tm, 2, dff), F32)
        state_spec = pl.BlockSpec((1, 2, tf), lambda i, j: (i, 0, j))
        scratch += [pltpu.VMEM((tm + 8, tf), F32), pltpu.VMEM((tm + 8, tf), F32),
                    pltpu.VMEM((2, 8, dff), F32)]
    else:
        assert m // tm == 1 and tm % seq_rows == 0
        in_specs += [pl.BlockSpec((2, tm, tf), lambda i, j: (0, 0, j)),
                     pl.BlockSpec((2, tm, tf), lambda i, j: (0, 0, nf + j))]
        args += [prev, prev]
        state = jax.ShapeDtypeStruct((m, dff), F32)
        state_spec = pl.BlockSpec((tm, tf), lambda i, j: (i, j))
    return pl.pallas_call(
        functools.partial(_ffn_kernel, tm=tm, tf=tf, seq_rows=seq_rows),
        grid=(m // tm, nf),
        in_specs=in_specs,
        out_specs=[pl.BlockSpec((tm, d), row), state_spec, state_spec],
        out_shape=[jax.ShapeDtypeStruct((m, d), F32), state, state],
        scratch_shapes=scratch,
        compiler_params=_params("arbitrary", "arbitrary"),
        name="ffn",
    )(*args)


def _alibi_slopes():
    n = N_GROUPS * HPG
    s = 2.0 ** (-8.0 * np.arange(1, n + 1, dtype=np.float32) / n)
    return jnp.asarray(s, F32).reshape(N_GROUPS, HPG)


def _kv_rows(z, g, rows):
    k = z[SLOT_K + g * HPG:SLOT_K + (g + 1) * HPG, rows]
    v = z[SLOT_V + g * HPG:SLOT_V + (g + 1) * HPG, rows]
    return jnp.stack([k, v], axis=0).transpose(2, 0, 1, 3)


def kernel(x_prompt, x_sample, state_hgrn, cache_kv_g0, cache_kv_g1, cache_kv_g2, state_ffn_conv,
           norm_mix_w, w_in, lb_logits, hgrn_norm_w, w_proj_a, w_proj_b, w_out, norm_ffn_w,
           w_up, ffn_conv_w, ffn_conv_b, w_down, norm_final_w):
    assert w_in.shape[0] == 1, "single-layer trunk"
    _, t, d = x_prompt.shape
    n_seq, n_new, _ = x_sample.shape
    dff = w_down.shape[1]
    slopes = _alibi_slopes()
    lb = jnp.cumsum(jax.nn.softmax(lb_logits.astype(F32), axis=0), axis=0)[0].reshape(N_HA, 1, LANES)
    hnw = hgrn_norm_w[0].reshape(N_HA, 1, LANES)
    nmw = norm_mix_w[0].reshape(1, d)
    nfw = norm_ffn_w[0].reshape(1, d)
    fw = norm_final_w.reshape(1, d)
    n_a, n_qkv = 4 * N_HA * LANES, 3 * N_GROUPS * HPG * LANES
    w_in_b = jnp.concatenate([w_in[0, :, :n_a], w_in[0, :, n_a + n_qkv:], w_in[0, :, n_a:n_a + n_qkv]],
                             axis=1).astype(BF16)
    w_a_b = w_proj_a[0].astype(BF16)
    w_b_b = w_proj_b[0].astype(BF16)
    w_o_b = w_out[0].astype(BF16)
    w_up_b = w_up[0].astype(BF16)
    w_dn_b = w_down[0].astype(BF16)
    cw = ffn_conv_w[0]
    cb = ffn_conv_b[0].reshape(1, 2 * dff)
    caches = (cache_kv_g0, cache_kv_g1, cache_kv_g2)

    xp = x_prompt[0]
    z = _inproj(xp, nmw, w_in_b, 1024, 1280)
    o_a, s_p = _hgrn(z, jnp.zeros((1, N_HA, LANES, LANES), F32), lb, hnw,
                     n_seq=1, rows_per_seq=t, chunk=64, valid=64, rows_per_step=512)
    o_b = _attn_prompt(z, slopes)
    x1 = _mix(o_a, o_b, z, xp, w_a_b, w_b_b, w_o_b, 256)
    y_p, sg, sa = _ffn(x1, nfw, w_up_b, cw, cb, w_dn_b, fw, tm=512, tf=512)
    conv_p = jnp.concatenate([sg[-1], sa[-1]], axis=1)[None, None]
    kv_p = [_kv_rows(z, g, slice(t - min(win, t), t))[None, None] for g, (win, _) in enumerate(PATTERNS)]

    ms = n_seq * n_new
    xs = x_sample.reshape(ms, d)
    z = _inproj(xs, nmw, w_in_b, ms, 1280)
    pad = 8
    z_a_pad = jnp.pad(z[:SLOT_GATE].reshape(SLOT_GATE, n_seq, n_new, LANES),
                      ((0, 0), (0, 0), (0, pad - n_new), (0, 0)))
    o_a, s_s = _hgrn(z_a_pad.reshape(SLOT_GATE, n_seq * pad, LANES), state_hgrn[0], lb, hnw,
                     n_seq=n_seq, rows_per_seq=pad, chunk=pad, valid=n_new, rows_per_step=pad)
    o_a = o_a.reshape(N_HA, n_seq, pad, LANES)[:, :, :n_new].reshape(N_HA, ms, LANES)
    o_b = _attn_sample(z[SLOT_Q:], caches, slopes, n_seq, n_new).reshape(ms, HPG * LANES).astype(BF16)
    x1 = _mix(o_a, o_b, z, xs, w_a_b, w_b_b, w_o_b, ms)
    prev = jnp.repeat(state_ffn_conv[0].transpose(1, 0, 2), n_new, axis=1)
    y_s, ug, ua = _ffn(x1, nfw, w_up_b, cw, cb, w_dn_b, fw, tm=ms, tf=512, seq_rows=n_new, prev=prev)
    u = jnp.concatenate([ug, ua], axis=1).reshape(n_seq, n_new, 2 * dff)
    conv_s = u[:, n_new - 2:][None]
    kv_s = []
    for g in range(N_GROUPS):
        new = _kv_rows(z, g, slice(0, ms)).reshape(n_seq, n_new, 2, HPG, LANES)
        length = caches[g].shape[2]
        kv_s.append(jnp.concatenate([caches[g][0], new], axis=1)[:, -length:][None])

    return (y_p[None], y_s.reshape(n_seq, n_new, d), s_p[None], s_s[None],
            kv_p[0], kv_s[0], kv_p[1], kv_s[1], kv_p[2], kv_s[2], conv_p, conv_s)
```

```python
import functools

import numpy as np
import jax
import jax.numpy as jnp
from jax import lax
from jax.experimental import pallas as pl
from jax.experimental.pallas import tpu as pltpu

F32 = jnp.float32
BF16 = jnp.bfloat16

LANES = 128
EPS = 1e-6
NEG = -1e30
VMEM_LIMIT = 56 * 1024 * 1024

N_HA = 8
N_GROUPS = 3
HPG = 4
PATTERNS = ((128, 1), (512, 4), (2048, 16))
NKEYS = 128
SLOT_A = 0
SLOT_GATE = 32
SLOT_Q = 64
SLOT_K = 76
SLOT_V = 88


def _dot(a, b):
    return jnp.dot(a, b, preferred_element_type=F32)


def _dot_nt(a, b):
    return lax.dot_general(a, b, (((1,), (1,)), ((), ())), preferred_element_type=F32)


def _dot_tn(a, b):
    return lax.dot_general(a, b, (((0,), (0,)), ((), ())), preferred_element_type=F32)


def _sigmoid(x):
    return 1.0 / (1.0 + jnp.exp(-x))


def _params(*sem):
    return pltpu.CompilerParams(dimension_semantics=sem, vmem_limit_bytes=VMEM_LIMIT)


def _inproj_kernel(x_ref, nw_ref, w_ref, o_ref, h_ref):
    @pl.when(pl.program_id(1) == 0)
    def _():
        x = x_ref[...]
        ms = jnp.mean(x * x, axis=-1, keepdims=True)
        h_ref[...] = (x * lax.rsqrt(ms + EPS) * nw_ref[...]).astype(BF16)

    z = _dot(h_ref[...], w_ref[...])
    for c in range(o_ref.shape[0]):
        o_ref[c] = z[:, c * LANES:(c + 1) * LANES].astype(o_ref.dtype)


def _inproj(x, nw, w, tm, tn):
    m, d = x.shape
    n_slots = w.shape[1] // LANES
    spt = tn // LANES
    assert m % tm == 0 and n_slots % spt == 0
    return pl.pallas_call(
        _inproj_kernel,
        grid=(m // tm, n_slots // spt),
        in_specs=[
            pl.BlockSpec((tm, d), lambda i, j: (i, 0)),
            pl.BlockSpec((1, d), lambda i, j: (0, 0)),
            pl.BlockSpec((d, tn), lambda i, j: (0, j)),
        ],
        out_specs=pl.BlockSpec((spt, tm, LANES), lambda i, j: (j, i, 0)),
        out_shape=jax.ShapeDtypeStruct((n_slots, m, LANES), F32),
        scratch_shapes=[pltpu.VMEM((tm, d), BF16)],
        compiler_params=_params("parallel", "arbitrary"),
        name="inproj",
    )(x, nw, w)


HGRN_HEAD_BATCH = 8


def _hgrn_levels(c):
    levels, h = [], c // 2
    while h >= 1:
        levels.append(h)
        h //= 2
    return levels


def _hgrn_matrix(c):
    r = np.arange(c)
    low = (r[None, :] <= r[:, None]).astype(np.float32)
    mats = [low, 1.0 - low]
    for h in _hgrn_levels(c)[:-1]:
        mid = (r // (2 * h)) * (2 * h) + h
        ref = (r[None, :] <= (mid - 1)[:, None]).astype(np.float32)
        mats.append(low - ref)
    return np.concatenate(mats, axis=0)


def _hgrn_kernel(zq_ref, zf_ref, zi_ref, zg_ref, s0_ref, lb_ref, nw_ref, a_ref, o_ref, sout_ref,
                 st_ref, *, chunk, valid, n_chunks):
    c = chunk
    levels = _hgrn_levels(c)
    n = pl.program_id(1)

    @pl.when(n == 0)
    def _():
        for h in range(N_HA):
            st_ref[h] = s0_ref[0, h].T

    row = lax.broadcasted_iota(jnp.int32, (c, 1), 0)
    ti = lax.broadcasted_iota(jnp.int32, (c, c), 0)
    si = lax.broadcasted_iota(jnp.int32, (c, c), 1)
    eye = ti == si
    pair = [((ti // (2 * h)) == (si // (2 * h))) & ((ti % (2 * h)) >= h) & ((si % (2 * h)) < h) for h in levels]
    a_mat = a_ref[...]

    n_hb = N_HA // HGRN_HEAD_BATCH

    def one_chunk(it, carry):
        r0 = pl.multiple_of((it // n_hb) * c, c)
        if n_hb == 1:
            heads = list(range(N_HA))
        else:
            heads = [(it % n_hb) * HGRN_HEAD_BATCH + hh for hh in range(HGRN_HEAD_BATCH)]

        def gates(h):
            zq = zq_ref[h, pl.ds(r0, c), :].astype(F32)
            zf = zf_ref[h, pl.ds(r0, c), :].astype(F32)
            v = zi_ref[h, pl.ds(r0, c), :].astype(F32)
            zg = zg_ref[h, pl.ds(r0, c), :].astype(F32)
            lb = lb_ref[h]
            f = lb + (1.0 - lb) * _sigmoid(zf)
            logf = jnp.log(f)
            k = 1.0 - f
            if valid < c:
                live = row < valid
                f = jnp.where(live, f, 1.0)
                logf = jnp.where(live, logf, 0.0)
                k = jnp.where(live, k, 0.0)
            q = zq * _sigmoid(zq)
            hi = logf.astype(BF16)
            lo = (logf - hi.astype(F32)).astype(BF16)
            rs = _dot(a_mat, jnp.concatenate([hi, lo], axis=1))
            return q, k, f, v, zg, rs[:, :LANES] + rs[:, LANES:]

        def mix_tokens(h, q, k, f, v, zg, rs):
            cum = rs[0:c]
            rev = rs[c:2 * c]
            total = cum[c - 1:c]
            st = st_ref[h]
            vb = v.astype(BF16)
            inter = _dot_nt((q * jnp.exp(cum)).astype(BF16), st.astype(BF16))
            scores = jnp.where(eye, _dot_nt(q.astype(BF16), k.astype(BF16)), 0.0)
            for li in range(len(levels)):
                if levels[li] == 1:
                    qe, ke = q * f, k
                else:
                    e = jnp.exp(-jnp.abs(rs[(2 + li) * c:(3 + li) * c]))
                    qe, ke = q * e, k * e
                scores = scores + jnp.where(pair[li], _dot_nt(qe.astype(BF16), ke.astype(BF16)), 0.0)
            o = inter + _dot(scores.astype(BF16), vb)
            kdec = (k * jnp.exp(rev)).astype(BF16)
            st_ref[h] = st * jnp.exp(total) + _dot_tn(vb, kdec)
            ms = jnp.mean(o * o, axis=-1, keepdims=True)
            on = o * lax.rsqrt(ms + EPS) * nw_ref[h] * (zg * _sigmoid(zg))
            o_ref[h, pl.ds(r0, c), :] = on.astype(o_ref.dtype)

        staged = [gates(h) for h in heads]
        for h, args in zip(heads, staged):
            mix_tokens(h, *args)
        return carry

    lax.fori_loop(0, n_chunks * n_hb, one_chunk, 0)

    @pl.when(n == pl.num_programs(1) - 1)
    def _():
        for h in range(N_HA):
            sout_ref[0, h] = st_ref[h].T


def _hgrn(z_a, s0, lb, nw, *, n_seq, rows_per_seq, chunk, valid, rows_per_step):
    m = z_a.shape[1]
    assert m == n_seq * rows_per_seq and rows_per_seq % rows_per_step == 0 and rows_per_step % chunk == 0
    nb = rows_per_seq // rows_per_step
    a_mat = jnp.asarray(_hgrn_matrix(chunk), BF16)
    zspec = lambda s: pl.BlockSpec((N_HA, rows_per_step, LANES), lambda b, n: (s, b * nb + n, 0))
    sspec = pl.BlockSpec((1, N_HA, LANES, LANES), lambda b, n: (b, 0, 0, 0))
    vspec = pl.BlockSpec((N_HA, 1, LANES), lambda b, n: (0, 0, 0))
    return pl.pallas_call(
        functools.partial(_hgrn_kernel, chunk=chunk, valid=valid, n_chunks=rows_per_step // chunk),
        grid=(n_seq, nb),
        in_specs=[zspec(0), zspec(1), zspec(2), zspec(3), sspec, vspec, vspec,
                  pl.BlockSpec(a_mat.shape, lambda b, n: (0, 0))],
        out_specs=[pl.BlockSpec((N_HA, rows_per_step, LANES), lambda b, n: (0, b * nb + n, 0)), sspec],
        out_shape=[jax.ShapeDtypeStruct((N_HA, m, LANES), BF16),
                   jax.ShapeDtypeStruct((n_seq, N_HA, LANES, LANES), F32)],
        scratch_shapes=[pltpu.VMEM((N_HA, LANES, LANES), F32)],
        compiler_params=_params("parallel", "arbitrary"),
        name="hgrn",
    )(z_a, z_a, z_a, z_a, s0, lb, nw, a_mat)


ATT_TILE = 2048
ATT_BATCH = 16


def _band_softmax(q, kp, ko, vp, vo, bias_p, bias_o, scale):
    qb = (q * scale).astype(BF16)
    ones = jnp.ones(vp.shape, BF16)
    sp = _dot_nt(qb, kp.astype(BF16)) + bias_p
    so = _dot_nt(qb, ko.astype(BF16)) + bias_o
    m = jnp.maximum(jnp.max(sp, axis=-1, keepdims=True), jnp.max(so, axis=-1, keepdims=True))
    pp = jnp.exp(sp - m).astype(BF16)
    po = jnp.exp(so - m).astype(BF16)
    acc = (_dot(pp, jnp.concatenate([vp.astype(BF16), ones], axis=1))
           + _dot(po, jnp.concatenate([vo.astype(BF16), ones], axis=1)))
    den = acc[:, LANES:]
    return acc[:, :LANES] / den, m + jnp.log(den)


def _attn_prompt_kernel(slopes_ref, *refs):
    ins, o_ref, scr, bias_ref = refs[:15], refs[15], refs[16:22], refs[22]
    h = pl.program_id(0)
    n = pl.program_id(1)
    blk = NKEYS
    scale = float(LANES) ** -0.5
    ti = lax.broadcasted_iota(jnp.int32, (blk, blk), 0)
    si = lax.broadcasted_iota(jnp.int32, (blk, blk), 1)
    dist_o = (ti - si).astype(F32)
    dist_p = dist_o + float(blk)
    for g, (_, dil) in enumerate(PATTERNS):
        q_ref, ko_ref, kp_ref, vo_ref, vp_ref = ins[5 * g:5 * g + 5]
        og_ref, lg_ref = scr[2 * g], scr[2 * g + 1]
        slope_d = slopes_ref[g, h] * float(dil)
        span = blk * dil
        nj = ATT_TILE // span
        bias_prev = jnp.where(dist_p <= float(blk), -slope_d * dist_p, NEG)
        bias_ref[3 * g] = jnp.where(dist_o >= 0.0, -slope_d * dist_o, NEG)
        bias_ref[3 * g + 1] = bias_prev
        bias_ref[3 * g + 2] = jnp.where(n > 0, bias_prev, NEG)

        def unit(start, prev_start, from_prev_tile, g=g, dil=dil, refs5=(q_ref, ko_ref, kp_ref, vo_ref, vp_ref),
                 outs=(og_ref, lg_ref)):
            q_ref, ko_ref, kp_ref, vo_ref, vp_ref = refs5
            rows = pl.ds(start, blk, stride=dil) if dil > 1 else pl.ds(start, blk)
            prows = pl.ds(prev_start, blk, stride=dil) if dil > 1 else pl.ds(prev_start, blk)
            kp = kp_ref[0, prows, :] if from_prev_tile else ko_ref[0, prows, :]
            vp = vp_ref[0, prows, :] if from_prev_tile else vo_ref[0, prows, :]
            bias_p = bias_ref[3 * g + 2] if from_prev_tile else bias_ref[3 * g + 1]
            o, lse = _band_softmax(q_ref[0, rows, :], kp, ko_ref[0, rows, :], vp, vo_ref[0, rows, :],
                                   bias_p, bias_ref[3 * g], scale)
            outs[0][rows, :] = o
            outs[1][rows, :] = lse

        n_batches = nj * dil // ATT_BATCH

        def batch(b, first_tile_block):
            for i in range(ATT_BATCH):
                u = b * ATT_BATCH + i
                j, r = u // dil, u % dil
                if first_tile_block:
                    unit(r, r, True)
                else:
                    unit(j * span + r, (j - 1) * span + r, False)

        b0 = 0
        if dil < ATT_BATCH:
            for u in range(ATT_BATCH):
                j, r = u // dil, u % dil
                if j == 0:
                    unit(r, r, True)
                else:
                    unit(j * span + r, (j - 1) * span + r, False)
            b0 = 1
        n_first = max(dil // ATT_BATCH, b0)

        def first_batches(b, carry):
            batch(b, True)
            return carry

        def later_batches(b, carry):
            batch(b, False)
            return carry

        if n_first > b0:
            lax.fori_loop(b0, n_first, first_batches, 0)
        if n_batches > n_first:
            lax.fori_loop(n_first, n_batches, later_batches, 0)

    rows_per_pass = 256

    def merge(c, carry):
        rows = pl.ds(pl.multiple_of(c * rows_per_pass, rows_per_pass), rows_per_pass)
        l0, l1, l2 = scr[1][rows, :], scr[3][rows, :], scr[5][rows, :]
        mx = jnp.maximum(jnp.maximum(l0, l1), l2)
        w0, w1, w2 = jnp.exp(l0 - mx), jnp.exp(l1 - mx), jnp.exp(l2 - mx)
        num = w0 * scr[0][rows, :] + w1 * scr[2][rows, :] + w2 * scr[4][rows, :]
        o_ref[rows, :] = (num / (w0 + w1 + w2)).astype(o_ref.dtype)
        return carry

    lax.fori_loop(0, ATT_TILE // rows_per_pass, merge, 0)


def _attn_prompt(z, slopes):
    t = z.shape[1]
    assert t % ATT_TILE == 0
    in_specs = [pl.BlockSpec(memory_space=pltpu.SMEM)]
    args = [slopes]
    for g, (win, dil) in enumerate(PATTERNS):
        assert win // dil == NKEYS
        prev_rows = NKEYS * dil
        ratio = ATT_TILE // prev_rows
        own = lambda base, g=g: pl.BlockSpec((1, ATT_TILE, LANES), lambda h, n: (base + g * HPG + h, n, 0))
        prev = lambda base, g=g, ratio=ratio, prev_rows=prev_rows: pl.BlockSpec(
            (1, prev_rows, LANES), lambda h, n: (base + g * HPG + h, jnp.maximum(n * ratio - 1, 0), 0))
        in_specs += [own(SLOT_Q), own(SLOT_K), prev(SLOT_K), own(SLOT_V), prev(SLOT_V)]
        args += [z] * 5
    return pl.pallas_call(
        _attn_prompt_kernel,
        grid=(HPG, t // ATT_TILE),
        in_specs=in_specs,
        out_specs=pl.BlockSpec((ATT_TILE, LANES), lambda h, n: (n, h)),
        out_shape=jax.ShapeDtypeStruct((t, HPG * LANES), BF16),
        scratch_shapes=[pltpu.VMEM((ATT_TILE, LANES), F32)] * 6 + [pltpu.VMEM((9, NKEYS, NKEYS), F32)],
        compiler_params=_params("parallel", "arbitrary"),
        name="attn_prompt",
    )(*args)


def _attn_sample_kernel(slopes_ref, z_ref, c0_ref, c1_ref, c2_ref, o_ref, *scr, n_new):
    caches = (c0_ref, c1_ref, c2_ref)
    bufs, sem = scr[:3], scr[3]
    b = pl.program_id(0)
    slot = b % 2
    s = n_new
    scale = float(LANES) ** -0.5

    def copies(seq, sl):
        out = []
        for g in range(N_GROUPS):
            for kv in range(2):
                for h in range(HPG):
                    out.append(pltpu.make_async_copy(caches[g].at[0, seq, :, kv, h, :], bufs[g].at[sl, kv, h],
                                                     sem.at[sl, g, kv, h]))
        return out

    @pl.when(b == 0)
    def _():
        for cp in copies(0, 0):
            cp.start()

    @pl.when(b + 1 < pl.num_programs(0))
    def _():
        for cp in copies(b + 1, 1 - slot):
            cp.start()

    for cp in copies(b, slot):
        cp.wait()

    for h in range(HPG):
        outs, lses = [], []
        for g, (win, dil) in enumerate(PATTERNS):
            length = bufs[g].shape[3]
            q = z_ref[g * HPG + h, 0]
            kn = z_ref[12 + g * HPG + h, 0]
            vn = z_ref[24 + g * HPG + h, 0]
            slope = slopes_ref[g, h]
            qb = q.astype(BF16)
            qi = lax.broadcasted_iota(jnp.int32, (s, length), 0)
            ri = lax.broadcasted_iota(jnp.int32, (s, length), 1)
            dist = length + qi - ri
            ok = (dist <= win) & ((dist & (dil - 1)) == 0)
            sc = _dot_nt(qb, bufs[g][slot, 0, h].astype(BF16)) * scale - slope * dist.astype(F32)
            sc = jnp.where(ok, sc, NEG)
            qn = lax.broadcasted_iota(jnp.int32, (s, s), 0)
            rn = lax.broadcasted_iota(jnp.int32, (s, s), 1)
            dn = qn - rn
            okn = (dn >= 0) & ((dn & (dil - 1)) == 0)
            sn = _dot_nt(qb, kn.astype(BF16)) * scale - slope * dn.astype(F32)
            sn = jnp.where(okn, sn, NEG)
            m = jnp.maximum(jnp.max(sc, axis=-1, keepdims=True), jnp.max(sn, axis=-1, keepdims=True))
            pc = jnp.exp(sc - m)
            pn = jnp.exp(sn - m)
            den = jnp.sum(pc, axis=-1, keepdims=True) + jnp.sum(pn, axis=-1, keepdims=True)
            acc = (_dot(pc.astype(BF16), bufs[g][slot, 1, h].astype(BF16))
                   + _dot(pn.astype(BF16), vn.astype(BF16)))
            outs.append(acc / den)
            lses.append(m + jnp.log(den))
        mx = jnp.maximum(jnp.maximum(lses[0], lses[1]), lses[2])
        ws = [jnp.exp(l - mx) for l in lses]
        num = ws[0] * outs[0] + ws[1] * outs[1] + ws[2] * outs[2]
        o_ref[0, :, h * LANES:(h + 1) * LANES] = num / (ws[0] + ws[1] + ws[2])


def _attn_sample(z_qkv, caches, slopes, n_seq, n_new):
    z4 = z_qkv.reshape(3 * N_GROUPS * HPG, n_seq, n_new, LANES)
    for g, (win, _) in enumerate(PATTERNS):
        assert caches[g].shape[2] == win, "window buffers shorter than the window are not supported"
    bufs = [pltpu.VMEM((2, 2, HPG, c.shape[2], LANES), F32) for c in caches]
    return pl.pallas_call(
        functools.partial(_attn_sample_kernel, n_new=n_new),
        grid=(n_seq,),
        in_specs=[pl.BlockSpec(memory_space=pltpu.SMEM),
                  pl.BlockSpec((z4.shape[0], 1, n_new, LANES), lambda b: (0, b, 0, 0)),
                  pl.BlockSpec(memory_space=pl.ANY), pl.BlockSpec(memory_space=pl.ANY),
                  pl.BlockSpec(memory_space=pl.ANY)],
        out_specs=pl.BlockSpec((1, n_new, HPG * LANES), lambda b: (b, 0, 0)),
        out_shape=jax.ShapeDtypeStruct((n_seq, n_new, HPG * LANES), F32),
        scratch_shapes=bufs + [pltpu.SemaphoreType.DMA((2, N_GROUPS, 2, HPG))],
        compiler_params=_params("arbitrary"),
        name="attn_sample",
    )(slopes, z4, *caches)


def _mix_kernel(oa_ref, ob_ref, ga_ref, gb_ref, x_ref, wa_ref, wb_ref, wo_ref, o_ref):
    oa = jnp.concatenate([oa_ref[h] for h in range(oa_ref.shape[0])], axis=1)
    ya = _dot(oa, wa_ref[...])
    yb = _dot(ob_ref[...], wb_ref[...])
    parts = []
    for c in range(ga_ref.shape[0]):
        cols = slice(c * LANES, (c + 1) * LANES)
        mix = _sigmoid(ga_ref[c].astype(F32)) * ya[:, cols] + _sigmoid(gb_ref[c].astype(F32)) * yb[:, cols]
        parts.append(mix.astype(BF16))
    o_ref[...] = x_ref[...] + _dot(jnp.concatenate(parts, axis=1), wo_ref[...])


def _mix(o_a, o_b, z_gate, x, w_a, w_b, w_o, tm):
    m, d = x.shape
    ng = d // LANES
    assert m % tm == 0
    full = lambda a: pl.BlockSpec(a.shape, lambda i: (0, 0))
    return pl.pallas_call(
        _mix_kernel,
        grid=(m // tm,),
        in_specs=[pl.BlockSpec((N_HA, tm, LANES), lambda i: (0, i, 0)),
                  pl.BlockSpec((tm, o_b.shape[1]), lambda i: (i, 0)),
                  pl.BlockSpec((ng, tm, LANES), lambda i: (SLOT_GATE // ng, i, 0)),
                  pl.BlockSpec((ng, tm, LANES), lambda i: (SLOT_GATE // ng + 1, i, 0)),
                  pl.BlockSpec((tm, d), lambda i: (i, 0)),
                  full(w_a), full(w_b), full(w_o)],
        out_specs=pl.BlockSpec((tm, d), lambda i: (i, 0)),
        out_shape=jax.ShapeDtypeStruct((m, d), F32),
        compiler_params=_params("parallel"),
        name="mix",
    )(o_a, o_b, z_gate, z_gate, x, w_a, w_b, w_o)


def _ffn_kernel(*refs, tm, tf, seq_rows):
    (x_ref, nw_ref, wug_ref, wua_ref, cwg_ref, cwa_ref, cbg_ref, cba_ref, wd_ref, fw_ref) = refs[:10]
    if seq_rows is None:
        y_ref, sg_ref, sa_ref, h_ref, acc_ref, ug_ref, ua_ref, carry_ref = refs[10:]
    else:
        pg_ref, pa_ref, y_ref, sg_ref, sa_ref, h_ref, acc_ref = refs[10:]
    i = pl.program_id(0)
    j = pl.program_id(1)

    @pl.when(j == 0)
    def _():
        x = x_ref[...]
        ms = jnp.mean(x * x, axis=-1, keepdims=True)
        h_ref[...] = (x * lax.rsqrt(ms + EPS) * nw_ref[...]).astype(BF16)
        acc_ref[...] = jnp.zeros_like(acc_ref)

    if seq_rows is None:
        @pl.when((i == 0) & (j == 0))
        def _():
            carry_ref[...] = jnp.zeros_like(carry_ref)

    def conv(part, wu_ref, cw_ref, cb_ref):
        u = _dot(h_ref[...], wu_ref[...])
        if seq_rows is None:
            ub_ref, s_ref = ((ug_ref, sg_ref), (ua_ref, sa_ref))[part]
            cols = pl.ds(pl.multiple_of(j * tf, tf), tf)
            ub_ref[pl.ds(0, 8), :] = carry_ref[part, :, cols]
            ub_ref[pl.ds(8, tm), :] = u
            u1 = ub_ref[pl.ds(7, tm), :]
            u2 = ub_ref[pl.ds(6, tm), :]
            carry_ref[part, :, cols] = ub_ref[pl.ds(tm, 8), :]
            s_ref[0] = ub_ref[pl.ds(tm + 6, 2), :]
        else:
            p_ref, s_ref = ((pg_ref, sg_ref), (pa_ref, sa_ref))[part]
            pos = lax.broadcasted_iota(jnp.int32, (tm, 1), 0) % seq_rows
            u1 = jnp.where(pos == 0, p_ref[1], pltpu.roll(u, 1, 0))
            u2 = jnp.where(pos == 0, p_ref[0], jnp.where(pos == 1, p_ref[1], pltpu.roll(u, 2, 0)))
            s_ref[...] = u
        return cb_ref[...] + cw_ref[2:3, :] * u + cw_ref[1:2, :] * u1 + cw_ref[0:1, :] * u2

    cg = conv(0, wug_ref, cwg_ref, cbg_ref)
    ca = conv(1, wua_ref, cwa_ref, cba_ref)
    act = (cg * _sigmoid(cg) * ca).astype(BF16)
    acc_ref[...] += _dot(act, wd_ref[...])

    @pl.when(j == pl.num_programs(1) - 1)
    def _():
        x2 = x_ref[...] + acc_ref[...]
        ms = jnp.mean(x2 * x2, axis=-1, keepdims=True)
        y_ref[...] = x2 * lax.rsqrt(ms + EPS) * fw_ref[...]


def _ffn(x1, nw, w_up, conv_w, conv_b, w_down, fw, *, tm, tf, seq_rows=None, prev=None):
    m, d = x1.shape
    dff = w_down.shape[0]
    assert m % tm == 0 and dff % tf == 0
    nf = dff // tf
    row = lambda i, j: (i, 0)
    const = lambda i, j: (0, 0)
    gcol = lambda i, j: (0, j)
    acol = lambda i, j: (0, nf + j)
    in_specs = [pl.BlockSpec((tm, d), row), pl.BlockSpec((1, d), const),
                pl.BlockSpec((d, tf), gcol), pl.BlockSpec((d, tf), acol),
                pl.BlockSpec((3, tf), gcol), pl.BlockSpec((3, tf), acol),
                pl.BlockSpec((1, tf), gcol), pl.BlockSpec((1, tf), acol),
                pl.BlockSpec((tf, d), lambda i, j: (j, 0)), pl.BlockSpec((1, d), const)]
    args = [x1, nw, w_up, w_up, conv_w, conv_w, conv_b, conv_b, w_down, fw]
    scratch = [pltpu.VMEM((tm, d), BF16), pltpu.VMEM((tm, d), F32)]
    if seq_rows is None:
        state = jax.ShapeDtypeStruct((m // tm, 2, dff), F32)
        state_spec = pl.BlockSpec((1, 2, tf), lambda i, j: (i, 0, j))
        scratch += [pltpu.VMEM((tm + 8, tf), F32), pltpu.VMEM((tm + 8, tf), F32),
                    pltpu.VMEM((2, 8, dff), F32)]
    else:
        assert m // tm == 1 and tm % seq_rows == 0
        in_specs += [pl.BlockSpec((2, tm, tf), lambda i, j: (0, 0, j)),
                     pl.BlockSpec((2, tm, tf), lambda i, j: (0, 0, nf + j))]
        args += [prev, prev]
        state = jax.ShapeDtypeStruct((m, dff), F32)
        state_spec = pl.BlockSpec((tm, tf), lambda i, j: (i, j))
    return pl.pallas_call(
        functools.partial(_ffn_kernel, tm=tm, tf=tf, seq_rows=seq_rows),
        grid=(m // tm, nf),
        in_specs=in_specs,
        out_specs=[pl.BlockSpec((tm, d), row), state_spec, state_spec],
        out_shape=[jax.ShapeDtypeStruct((m, d), F32), state, state],
        scratch_shapes=scratch,
        compiler_params=_params("arbitrary", "arbitrary"),
        name="ffn",
    )(*args)


def _alibi_slopes():
    n = N_GROUPS * HPG
    s = 2.0 ** (-8.0 * np.arange(1, n + 1, dtype=np.float32) / n)
    return jnp.asarray(s, F32).reshape(N_GROUPS, HPG)


def _kv_rows(z, g, rows):
    k = z[SLOT_K + g * HPG:SLOT_K + (g + 1) * HPG, rows]
    v = z[SLOT_V + g * HPG:SLOT_V + (g + 1) * HPG, rows]
    return jnp.stack([k, v], axis=0).transpose(2, 0, 1, 3)


def kernel(x_prompt, x_sample, state_hgrn, cache_kv_g0, cache_kv_g1, cache_kv_g2, state_ffn_conv,
           norm_mix_w, w_in, lb_logits, hgrn_norm_w, w_proj_a, w_proj_b, w_out, norm_ffn_w,
           w_up, ffn_conv_w, ffn_conv_b, w_down, norm_final_w):
    assert w_in.shape[0] == 1, "single-layer trunk"
    _, t, d = x_prompt.shape
    n_seq, n_new, _ = x_sample.shape
    dff = w_down.shape[1]
    slopes = _alibi_slopes()
    lb = jnp.cumsum(jax.nn.softmax(lb_logits.astype(F32), axis=0), axis=0)[0].reshape(N_HA, 1, LANES)
    hnw = hgrn_norm_w[0].reshape(N_HA, 1, LANES)
    nmw = norm_mix_w[0].reshape(1, d)
    nfw = norm_ffn_w[0].reshape(1, d)
    fw = norm_final_w.reshape(1, d)
    n_a, n_qkv = 4 * N_HA * LANES, 3 * N_GROUPS * HPG * LANES
    w_in_b = jnp.concatenate([w_in[0, :, :n_a], w_in[0, :, n_a + n_qkv:], w_in[0, :, n_a:n_a + n_qkv]],
                             axis=1).astype(BF16)
    w_a_b = w_proj_a[0].astype(BF16)
    w_b_b = w_proj_b[0].astype(BF16)
    w_o_b = w_out[0].astype(BF16)
    w_up_b = w_up[0].astype(BF16)
    w_dn_b = w_down[0].astype(BF16)
    cw = ffn_conv_w[0]
    cb = ffn_conv_b[0].reshape(1, 2 * dff)
    caches = (cache_kv_g0, cache_kv_g1, cache_kv_g2)

    xp = x_prompt[0]
    z = _inproj(xp, nmw, w_in_b, 1024, 1280)
    o_a, s_p = _hgrn(z, jnp.zeros((1, N_HA, LANES, LANES), F32), lb, hnw,
                     n_seq=1, rows_per_seq=t, chunk=64, valid=64, rows_per_step=512)
    o_b = _attn_prompt(z, slopes)
    x1 = _mix(o_a, o_b, z, xp, w_a_b, w_b_b, w_o_b, 256)
    y_p, sg, sa = _ffn(x1, nfw, w_up_b, cw, cb, w_dn_b, fw, tm=512, tf=512)
    conv_p = jnp.concatenate([sg[-1], sa[-1]], axis=1)[None, None]
    kv_p = [_kv_rows(z, g, slice(t - min(win, t), t))[None, None] for g, (win, _) in enumerate(PATTERNS)]

    ms = n_seq * n_new
    xs = x_sample.reshape(ms, d)
    z = _inproj(xs, nmw, w_in_b, ms, 1280)
    pad = 8
    z_a_pad = jnp.pad(z[:SLOT_GATE].reshape(SLOT_GATE, n_seq, n_new, LANES),
                      ((0, 0), (0, 0), (0, pad - n_new), (0, 0)))
    o_a, s_s = _hgrn(z_a_pad.reshape(SLOT_GATE, n_seq * pad, LANES), state_hgrn[0], lb, hnw,
                     n_seq=n_seq, rows_per_seq=pad, chunk=pad, valid=n_new, rows_per_step=pad)
    o_a = o_a.reshape(N_HA, n_seq, pad, LANES)[:, :, :n_new].reshape(N_HA, ms, LANES)
    o_b = _attn_sample(z[SLOT_Q:], caches, slopes, n_seq, n_new).reshape(ms, HPG * LANES).astype(BF16)
    x1 = _mix(o_a, o_b, z, xs, w_a_b, w_b_b, w_o_b, ms)
    prev = jnp.repeat(state_ffn_conv[0].transpose(1, 0, 2), n_new, axis=1)
    y_s, ug, ua = _ffn(x1, nfw, w_up_b, cw, cb, w_dn_b, fw, tm=ms, tf=512, seq_rows=n_new, prev=prev)
    u = jnp.concatenate([ug, ua], axis=1).reshape(n_seq, n_new, 2 * dff)
    conv_s = u[:, n_new - 2:][None]
    kv_s = []
    for g in range(N_GROUPS):
        new = _kv_rows(z, g, slice(0, ms)).reshape(n_seq, n_new, 2, HPG, LANES)
        length = caches[g].shape[2]
        kv_s.append(jnp.concatenate([caches[g][0], new], axis=1)[:, -length:][None])

    return (y_p[None], y_s.reshape(n_seq, n_new, d), s_p[None], s_s[None],
            kv_p[0], kv_s[0], kv_p[1], kv_s[1], kv_p[2], kv_s[2], conv_p, conv_s)
```

```python
import functools

import numpy as np
import jax
import jax.numpy as jnp
from jax import lax
from jax.experimental import pallas as pl
from jax.experimental.pallas import tpu as pltpu

F32 = jnp.float32
BF16 = jnp.bfloat16

LANES = 128
EPS = 1e-6
NEG = -1e30
VMEM_LIMIT = 56 * 1024 * 1024

N_HA = 8
N_GROUPS = 3
HPG = 4
PATTERNS = ((128, 1), (512, 4), (2048, 16))
NKEYS = 128
SLOT_A = 0
SLOT_GATE = 32
SLOT_Q = 64
SLOT_K = 76
SLOT_V = 88


def _dot(a, b):
    return jnp.dot(a, b, preferred_element_type=F32)


def _dot_nt(a, b):
    return lax.dot_general(a, b, (((1,), (1,)), ((), ())), preferred_element_type=F32)


def _dot_tn(a, b):
    return lax.dot_general(a, b, (((0,), (0,)), ((), ())), preferred_element_type=F32)


def _sigmoid(x):
    return 1.0 / (1.0 + jnp.exp(-x))


def _params(*sem):
    return pltpu.CompilerParams(dimension_semantics=sem, vmem_limit_bytes=VMEM_LIMIT)


def _inproj_kernel(x_ref, nw_ref, w_ref, o_ref, h_ref):
    @pl.when(pl.program_id(1) == 0)
    def _():
        x = x_ref[...]
        ms = jnp.mean(x * x, axis=-1, keepdims=True)
        h_ref[...] = (x * lax.rsqrt(ms + EPS) * nw_ref[...]).astype(BF16)

    z = _dot(h_ref[...], w_ref[...])
    for c in range(o_ref.shape[0]):
        o_ref[c] = z[:, c * LANES:(c + 1) * LANES].astype(o_ref.dtype)


def _inproj(x, nw, w, tm, tn):
    m, d = x.shape
    n_slots = w.shape[1] // LANES
    spt = tn // LANES
    assert m % tm == 0 and n_slots % spt == 0
    return pl.pallas_call(
        _inproj_kernel,
        grid=(m // tm, n_slots // spt),
        in_specs=[
            pl.BlockSpec((tm, d), lambda i, j: (i, 0)),
            pl.BlockSpec((1, d), lambda i, j: (0, 0)),
            pl.BlockSpec((d, tn), lambda i, j: (0, j)),
        ],
        out_specs=pl.BlockSpec((spt, tm, LANES), lambda i, j: (j, i, 0)),
        out_shape=jax.ShapeDtypeStruct((n_slots, m, LANES), F32),
        scratch_shapes=[pltpu.VMEM((tm, d), BF16)],
        compiler_params=_params("parallel", "arbitrary"),
        name="inproj",
    )(x, nw, w)


HGRN_HEAD_BATCH = 8


def _hgrn_levels(c):
    levels, h = [], c // 2
    while h >= 1:
        levels.append(h)
        h //= 2
    return levels


def _hgrn_matrix(c):
    r = np.arange(c)
    low = (r[None, :] <= r[:, None]).astype(np.float32)
    mats = [low, 1.0 - low]
    for h in _hgrn_levels(c)[:-1]:
        mid = (r // (2 * h)) * (2 * h) + h
        ref = (r[None, :] <= (mid - 1)[:, None]).astype(np.float32)
        mats.append(low - ref)
    return np.concatenate(mats, axis=0)


def _hgrn_kernel(zq_ref, zf_ref, zi_ref, zg_ref, s0_ref, lb_ref, nw_ref, a_ref, o_ref, sout_ref,
                 st_ref, *, chunk, valid, n_chunks):
    c = chunk
    levels = _hgrn_levels(c)
    n = pl.program_id(1)

    @pl.when(n == 0)
    def _():
        for h in range(N_HA):
            st_ref[h] = s0_ref[0, h].T

    row = lax.broadcasted_iota(jnp.int32, (c, 1), 0)
    ti = lax.broadcasted_iota(jnp.int32, (c, c), 0)
    si = lax.broadcasted_iota(jnp.int32, (c, c), 1)
    eye = ti == si
    pair = [((ti // (2 * h)) == (si // (2 * h))) & ((ti % (2 * h)) >= h) & ((si % (2 * h)) < h) for h in levels]
    a_mat = a_ref[...]

    n_hb = N_HA // HGRN_HEAD_BATCH

    def one_chunk(it, carry):
        r0 = pl.multiple_of((it // n_hb) * c, c)
        if n_hb == 1:
            heads = list(range(N_HA))
        else:
            heads = [(it % n_hb) * HGRN_HEAD_BATCH + hh for hh in range(HGRN_HEAD_BATCH)]

        def gates(h):
            zq = zq_ref[h, pl.ds(r0, c), :].astype(F32)
            zf = zf_ref[h, pl.ds(r0, c), :].astype(F32)
            v = zi_ref[h, pl.ds(r0, c), :].astype(F32)
            zg = zg_ref[h, pl.ds(r0, c), :].astype(F32)
            lb = lb_ref[h]
            f = lb + (1.0 - lb) * _sigmoid(zf)
            logf = jnp.log(f)
            k = 1.0 - f
            if valid < c:
                live = row < valid
                f = jnp.where(live, f, 1.0)
                logf = jnp.where(live, logf, 0.0)
                k = jnp.where(live, k, 0.0)
            q = zq * _sigmoid(zq)
            hi = logf.astype(BF16)
            lo = (logf - hi.astype(F32)).astype(BF16)
            rs = _dot(a_mat, jnp.concatenate([hi, lo], axis=1))
            return q, k, f, v, zg, rs[:, :LANES] + rs[:, LANES:]

        def mix_tokens(h, q, k, f, v, zg, rs):
            cum = rs[0:c]
            rev = rs[c:2 * c]
            total = cum[c - 1:c]
            st = st_ref[h]
            vb = v.astype(BF16)
            inter = _dot_nt((q * jnp.exp(cum)).astype(BF16), st.astype(BF16))
            scores = jnp.where(eye, _dot_nt(q.astype(BF16), k.astype(BF16)), 0.0)
            for li in range(len(levels)):
                if levels[li] == 1:
                    qe, ke = q * f, k
                else:
                    e = jnp.exp(-jnp.abs(rs[(2 + li) * c:(3 + li) * c]))
                    qe, ke = q * e, k * e
                scores = scores + jnp.where(pair[li], _dot_nt(qe.astype(BF16), ke.astype(BF16)), 0.0)
            o = inter + _dot(scores.astype(BF16), vb)
            kdec = (k * jnp.exp(rev)).astype(BF16)
            st_ref[h] = st * jnp.exp(total) + _dot_tn(vb, kdec)
            ms = jnp.mean(o * o, axis=-1, keepdims=True)
            on = o * lax.rsqrt(ms + EPS) * nw_ref[h] * (zg * _sigmoid(zg))
            o_ref[h, pl.ds(r0, c), :] = on.astype(o_ref.dtype)

        staged = [gates(h) for h in heads]
        for h, args in zip(heads, staged):
            mix_tokens(h, *args)
        return carry

    lax.fori_loop(0, n_chunks * n_hb, one_chunk, 0)

    @pl.when(n == pl.num_programs(1) - 1)
    def _():
        for h in range(N_HA):
            sout_ref[0, h] = st_ref[h].T


def _hgrn(z_a, s0, lb, nw, *, n_seq, rows_per_seq, chunk, valid, rows_per_step):
    m = z_a.shape[1]
    assert m == n_seq * rows_per_seq and rows_per_seq % rows_per_step == 0 and rows_per_step % chunk == 0
    nb = rows_per_seq // rows_per_step
    a_mat = jnp.asarray(_hgrn_matrix(chunk), BF16)
    zspec = lambda s: pl.BlockSpec((N_HA, rows_per_step, LANES), lambda b, n: (s, b * nb + n, 0))
    sspec = pl.BlockSpec((1, N_HA, LANES, LANES), lambda b, n: (b, 0, 0, 0))
    vspec = pl.BlockSpec((N_HA, 1, LANES), lambda b, n: (0, 0, 0))
    return pl.pallas_call(
        functools.partial(_hgrn_kernel, chunk=chunk, valid=valid, n_chunks=rows_per_step // chunk),
        grid=(n_seq, nb),
        in_specs=[zspec(0), zspec(1), zspec(2), zspec(3), sspec, vspec, vspec,
                  pl.BlockSpec(a_mat.shape, lambda b, n: (0, 0))],
        out_specs=[pl.BlockSpec((N_HA, rows_per_step, LANES), lambda b, n: (0, b * nb + n, 0)), sspec],
        out_shape=[jax.ShapeDtypeStruct((N_HA, m, LANES), BF16),
                   jax.ShapeDtypeStruct((n_seq, N_HA, LANES, LANES), F32)],
        scratch_shapes=[pltpu.VMEM((N_HA, LANES, LANES), F32)],
        compiler_params=_params("parallel", "arbitrary"),
        name="hgrn",
    )(z_a, z_a, z_a, z_a, s0, lb, nw, a_mat)


ATT_TILE = 2048
ATT_BATCH = 16


def _band_softmax(q, kp, ko, vp, vo, bias_p, bias_o, scale):
    qb = (q * scale).astype(BF16)
    ones = jnp.ones(vp.shape, BF16)
    sp = _dot_nt(qb, kp.astype(BF16)) + bias_p
    so = _dot_nt(qb, ko.astype(BF16)) + bias_o
    m = jnp.maximum(jnp.max(sp, axis=-1, keepdims=True), jnp.max(so, axis=-1, keepdims=True))
    pp = jnp.exp(sp - m).astype(BF16)
    po = jnp.exp(so - m).astype(BF16)
    acc = (_dot(pp, jnp.concatenate([vp.astype(BF16), ones], axis=1))
           + _dot(po, jnp.concatenate([vo.astype(BF16), ones], axis=1)))
    den = acc[:, LANES:]
    return acc[:, :LANES] / den, m + jnp.log(den)


def _attn_prompt_kernel(slopes_ref, *refs):
    ins, o_ref, scr, bias_ref = refs[:15], refs[15], refs[16:22], refs[22]
    h = pl.program_id(0)
    n = pl.program_id(1)
    blk = NKEYS
    scale = float(LANES) ** -0.5
    ti = lax.broadcasted_iota(jnp.int32, (blk, blk), 0)
    si = lax.broadcasted_iota(jnp.int32, (blk, blk), 1)
    dist_o = (ti - si).astype(F32)
    dist_p = dist_o + float(blk)
    for g, (_, dil) in enumerate(PATTERNS):
        q_ref, ko_ref, kp_ref, vo_ref, vp_ref = ins[5 * g:5 * g + 5]
        og_ref, lg_ref = scr[2 * g], scr[2 * g + 1]
        slope_d = slopes_ref[g, h] * float(dil)
        span = blk * dil
        nj = ATT_TILE // span
        bias_prev = jnp.where(dist_p <= float(blk), -slope_d * dist_p, NEG)
        bias_ref[3 * g] = jnp.where(dist_o >= 0.0, -slope_d * dist_o, NEG)
        bias_ref[3 * g + 1] = bias_prev
        bias_ref[3 * g + 2] = jnp.where(n > 0, bias_prev, NEG)

        def unit(start, prev_start, from_prev_tile, g=g, dil=dil, refs5=(q_ref, ko_ref, kp_ref, vo_ref, vp_ref),
                 outs=(og_ref, lg_ref)):
            q_ref, ko_ref, kp_ref, vo_ref, vp_ref = refs5
            rows = pl.ds(start, blk, stride=dil) if dil > 1 else pl.ds(start, blk)
            prows = pl.ds(prev_start, blk, stride=dil) if dil > 1 else pl.ds(prev_start, blk)
            kp = kp_ref[0, prows, :] if from_prev_tile else ko_ref[0, prows, :]
            vp = vp_ref[0, prows, :] if from_prev_tile else vo_ref[0, prows, :]
            bias_p = bias_ref[3 * g + 2] if from_prev_tile else bias_ref[3 * g + 1]
            o, lse = _band_softmax(q_ref[0, rows, :], kp, ko_ref[0, rows, :], vp, vo_ref[0, rows, :],
                                   bias_p, bias_ref[3 * g], scale)
            outs[0][rows, :] = o
            outs[1][rows, :] = lse

        n_batches = nj * dil // ATT_BATCH

        def batch(b, first_tile_block):
            for i in range(ATT_BATCH):
                u = b * ATT_BATCH + i
                j, r = u // dil, u % dil
                if first_tile_block:
                    unit(r, r, True)
                else:
                    unit(j * span + r, (j - 1) * span + r, False)

        b0 = 0
        if dil < ATT_BATCH:
            for u in range(ATT_BATCH):
                j, r = u // dil, u % dil
                if j == 0:
                    unit(r, r, True)
                else:
                    unit(j * span + r, (j - 1) * span + r, False)
            b0 = 1
        n_first = max(dil // ATT_BATCH, b0)

        def first_batches(b, carry):
            batch(b, True)
            return carry

        def later_batches(b, carry):
            batch(b, False)
            return carry

        if n_first > b0:
            lax.fori_loop(b0, n_first, first_batches, 0)
        if n_batches > n_first:
            lax.fori_loop(n_first, n_batches, later_batches, 0)

    rows_per_pass = 256

    def merge(c, carry):
        rows = pl.ds(pl.multiple_of(c * rows_per_pass, rows_per_pass), rows_per_pass)
        l0, l1, l2 = scr[1][rows, :], scr[3][rows, :], scr[5][rows, :]
        mx = jnp.maximum(jnp.maximum(l0, l1), l2)
        w0, w1, w2 = jnp.exp(l0 - mx), jnp.exp(l1 - mx), jnp.exp(l2 - mx)
        num = w0 * scr[0][rows, :] + w1 * scr[2][rows, :] + w2 * scr[4][rows, :]
        o_ref[rows, :] = (num / (w0 + w1 + w2)).astype(o_ref.dtype)
        return carry

    lax.fori_loop(0, ATT_TILE // rows_per_pass, merge, 0)


def _attn_prompt(z, slopes):
    t = z.shape[1]
    assert t % ATT_TILE == 0
    in_specs = [pl.BlockSpec(memory_space=pltpu.SMEM)]
    args = [slopes]
    for g, (win, dil) in enumerate(PATTERNS):
        assert win // dil == NKEYS
        prev_rows = NKEYS * dil
        ratio = ATT_TILE // prev_rows
        own = lambda base, g=g: pl.BlockSpec((1, ATT_TILE, LANES), lambda h, n: (base + g * HPG + h, n, 0))
        prev = lambda base, g=g, ratio=ratio, prev_rows=prev_rows: pl.BlockSpec(
            (1, prev_rows, LANES), lambda h, n: (base + g * HPG + h, jnp.maximum(n * ratio - 1, 0), 0))
        in_specs += [own(SLOT_Q), own(SLOT_K), prev(SLOT_K), own(SLOT_V), prev(SLOT_V)]
        args += [z] * 5
    return pl.pallas_call(
        _attn_prompt_kernel,
        grid=(HPG, t // ATT_TILE),
        in_specs=in_specs,
        out_specs=pl.BlockSpec((ATT_TILE, LANES), lambda h, n: (n, h)),
        out_shape=jax.ShapeDtypeStruct((t, HPG * LANES), BF16),
        scratch_shapes=[pltpu.VMEM((ATT_TILE, LANES), F32)] * 6 + [pltpu.VMEM((9, NKEYS, NKEYS), F32)],
        compiler_params=_params("parallel", "arbitrary"),
        name="attn_prompt",
    )(*args)


def _attn_sample_kernel(slopes_ref, z_ref, c0_ref, c1_ref, c2_ref, o_ref, *scr, n_new):
    caches = (c0_ref, c1_ref, c2_ref)
    bufs, sem = scr[:3], scr[3]
    b = pl.program_id(0)
    slot = b % 2
    s = n_new
    scale = float(LANES) ** -0.5

    def copies(seq, sl):
        out = []
        for g in range(N_GROUPS):
            for kv in range(2):
                for h in range(HPG):
                    out.append(pltpu.make_async_copy(caches[g].at[0, seq, :, kv, h, :], bufs[g].at[sl, kv, h],
                                                     sem.at[sl, g, kv, h]))
        return out

    @pl.when(b == 0)
    def _():
        for cp in copies(0, 0):
            cp.start()

    @pl.when(b + 1 < pl.num_programs(0))
    def _():
        for cp in copies(b + 1, 1 - slot):
            cp.start()

    for cp in copies(b, slot):
        cp.wait()

    for h in range(HPG):
        outs, lses = [], []
        for g, (win, dil) in enumerate(PATTERNS):
            length = bufs[g].shape[3]
            q = z_ref[g * HPG + h, 0]
            kn = z_ref[12 + g * HPG + h, 0]
            vn = z_ref[24 + g * HPG + h, 0]
            slope = slopes_ref[g, h]
            qb = q.astype(BF16)
            qi = lax.broadcasted_iota(jnp.int32, (s, length), 0)
            ri = lax.broadcasted_iota(jnp.int32, (s, length), 1)
            dist = length + qi - ri
            ok = (dist <= win) & ((dist & (dil - 1)) == 0)
            sc = _dot_nt(qb, bufs[g][slot, 0, h].astype(BF16)) * scale - slope * dist.astype(F32)
            sc = jnp.where(ok, sc, NEG)
            qn = lax.broadcasted_iota(jnp.int32, (s, s), 0)
            rn = lax.broadcasted_iota(jnp.int32, (s, s), 1)
            dn = qn - rn
            okn = (dn >= 0) & ((dn & (dil - 1)) == 0)
            sn = _dot_nt(qb, kn.astype(BF16)) * scale - slope * dn.astype(F32)
            sn = jnp.where(okn, sn, NEG)
            m = jnp.maximum(jnp.max(sc, axis=-1, keepdims=True), jnp.max(sn, axis=-1, keepdims=True))
            pc = jnp.exp(sc - m)
            pn = jnp.exp(sn - m)
            den = jnp.sum(pc, axis=-1, keepdims=True) + jnp.sum(pn, axis=-1, keepdims=True)
            acc = (_dot(pc.astype(BF16), bufs[g][slot, 1, h].astype(BF16))
                   + _dot(pn.astype(BF16), vn.astype(BF16)))
            outs.append(acc / den)
            lses.append(m + jnp.log(den))
        mx = jnp.maximum(jnp.maximum(lses[0], lses[1]), lses[2])
        ws = [jnp.exp(l - mx) for l in lses]
        num = ws[0] * outs[0] + ws[1] * outs[1] + ws[2] * outs[2]
        o_ref[0, :, h * LANES:(h + 1) * LANES] = num / (ws[0] + ws[1] + ws[2])


def _attn_sample(z_qkv, caches, slopes, n_seq, n_new):
    z4 = z_qkv.reshape(3 * N_GROUPS * HPG, n_seq, n_new, LANES)
    for g, (win, _) in enumerate(PATTERNS):
        assert caches[g].shape[2] == win, "window buffers shorter than the window are not supported"
    bufs = [pltpu.VMEM((2, 2, HPG, c.shape[2], LANES), F32) for c in caches]
    return pl.pallas_call(
        functools.partial(_attn_sample_kernel, n_new=n_new),
        grid=(n_seq,),
        in_specs=[pl.BlockSpec(memory_space=pltpu.SMEM),
                  pl.BlockSpec((z4.shape[0], 1, n_new, LANES), lambda b: (0, b, 0, 0)),
                  pl.BlockSpec(memory_space=pl.ANY), pl.BlockSpec(memory_space=pl.ANY),
                  pl.BlockSpec(memory_space=pl.ANY)],
        out_specs=pl.BlockSpec((1, n_new, HPG * LANES), lambda b: (b, 0, 0)),
        out_shape=jax.ShapeDtypeStruct((n_seq, n_new, HPG * LANES), F32),
        scratch_shapes=bufs + [pltpu.SemaphoreType.DMA((2, N_GROUPS, 2, HPG))],
        compiler_params=_params("arbitrary"),
        name="attn_sample",
    )(slopes, z4, *caches)


def _mix_kernel(oa_ref, ob_ref, ga_ref, gb_ref, x_ref, wa_ref, wb_ref, wo_ref, o_ref):
    oa = jnp.concatenate([oa_ref[h] for h in range(oa_ref.shape[0])], axis=1)
    ya = _dot(oa, wa_ref[...])
    yb = _dot(ob_ref[...], wb_ref[...])
    parts = []
    for c in range(ga_ref.shape[0]):
        cols = slice(c * LANES, (c + 1) * LANES)
        mix = _sigmoid(ga_ref[c].astype(F32)) * ya[:, cols] + _sigmoid(gb_ref[c].astype(F32)) * yb[:, cols]
        parts.append(mix.astype(BF16))
    o_ref[...] = x_ref[...] + _dot(jnp.concatenate(parts, axis=1), wo_ref[...])


def _mix(o_a, o_b, z_gate, x, w_a, w_b, w_o, tm):
    m, d = x.shape
    ng = d // LANES
    assert m % tm == 0
    full = lambda a: pl.BlockSpec(a.shape, lambda i: (0, 0), pipeline_mode=pl.Buffered(1))
    return pl.pallas_call(
        _mix_kernel,
        grid=(m // tm,),
        in_specs=[pl.BlockSpec((N_HA, tm, LANES), lambda i: (0, i, 0)),
                  pl.BlockSpec((tm, o_b.shape[1]), lambda i: (i, 0)),
                  pl.BlockSpec((ng, tm, LANES), lambda i: (SLOT_GATE // ng, i, 0)),
                  pl.BlockSpec((ng, tm, LANES), lambda i: (SLOT_GATE // ng + 1, i, 0)),
                  pl.BlockSpec((tm, d), lambda i: (i, 0)),
                  full(w_a), full(w_b), full(w_o)],
        out_specs=pl.BlockSpec((tm, d), lambda i: (i, 0)),
        out_shape=jax.ShapeDtypeStruct((m, d), F32),
        compiler_params=_params("parallel"),
        name="mix",
    )(o_a, o_b, z_gate, z_gate, x, w_a, w_b, w_o)


def _ffn_kernel(*refs, tm, tf, seq_rows):
    (x_ref, nw_ref, wug_ref, wua_ref, cwg_ref, cwa_ref, cbg_ref, cba_ref, wd_ref, fw_ref) = refs[:10]
    if seq_rows is None:
        y_ref, sg_ref, sa_ref, h_ref, acc_ref, ug_ref, ua_ref, carry_ref = refs[10:]
    else:
        pg_ref, pa_ref, y_ref, sg_ref, sa_ref, h_ref, acc_ref = refs[10:]
    i = pl.program_id(0)
    j = pl.program_id(1)

    @pl.when(j == 0)
    def _():
        x = x_ref[...]
        ms = jnp.mean(x * x, axis=-1, keepdims=True)
        h_ref[...] = (x * lax.rsqrt(ms + EPS) * nw_ref[...]).astype(BF16)
        acc_ref[...] = jnp.zeros_like(acc_ref)

    if seq_rows is None:
        @pl.when((i == 0) & (j == 0))
        def _():
            carry_ref[...] = jnp.zeros_like(carry_ref)

    def conv(part, wu_ref, cw_ref, cb_ref):
        u = _dot(h_ref[...], wu_ref[...])
        if seq_rows is None:
            ub_ref, s_ref = ((ug_ref, sg_ref), (ua_ref, sa_ref))[part]
            cols = pl.ds(pl.multiple_of(j * tf, tf), tf)
            ub_ref[pl.ds(0, 8), :] = carry_ref[part, :, cols]
            ub_ref[pl.ds(8, tm), :] = u
            u1 = ub_ref[pl.ds(7, tm), :]
            u2 = ub_ref[pl.ds(6, tm), :]
            carry_ref[part, :, cols] = ub_ref[pl.ds(tm, 8), :]
            s_ref[0] = ub_ref[pl.ds(tm + 6, 2), :]
        else:
            p_ref, s_ref = ((pg_ref, sg_ref), (pa_ref, sa_ref))[part]
            pos = lax.broadcasted_iota(jnp.int32, (tm, 1), 0) % seq_rows
            u1 = jnp.where(pos == 0, p_ref[1], pltpu.roll(u, 1, 0))
            u2 = jnp.where(pos == 0, p_ref[0], jnp.where(pos == 1, p_ref[1], pltpu.roll(u, 2, 0)))
            s_ref[...] = u
        return cb_ref[...] + cw_ref[2:3, :] * u + cw_ref[1:2, :] * u1 + cw_ref[0:1, :] * u2

    cg = conv(0, wug_ref, cwg_ref, cbg_ref)
    ca = conv(1, wua_ref, cwa_ref, cba_ref)
    act = (cg * _sigmoid(cg) * ca).astype(BF16)
    acc_ref[...] += _dot(act, wd_ref[...])

    @pl.when(j == pl.num_programs(1) - 1)
    def _():
        x2 = x_ref[...] + acc_ref[...]
        ms = jnp.mean(x2 * x2, axis=-1, keepdims=True)
        y_ref[...] = x2 * lax.rsqrt(ms + EPS) * fw_ref[...]


def _ffn(x1, nw, w_up, conv_w, conv_b, w_down, fw, *, tm, tf, seq_rows=None, prev=None):
    m, d = x1.shape
    dff = w_down.shape[0]
    assert m % tm == 0 and dff % tf == 0
    nf = dff // tf
    row = lambda i, j: (i, 0)
    const = lambda i, j: (0, 0)
    gcol = lambda i, j: (0, j)
    acol = lambda i, j: (0, nf + j)
    in_specs = [pl.BlockSpec((tm, d), row), pl.BlockSpec((1, d), const),
                pl.BlockSpec((d, tf), gcol), pl.BlockSpec((d, tf), acol),
                pl.BlockSpec((3, tf), gcol), pl.BlockSpec((3, tf), acol),
                pl.BlockSpec((1, tf), gcol), pl.BlockSpec((1, tf), acol),
                pl.BlockSpec((tf, d), lambda i, j: (j, 0)), pl.BlockSpec((1, d), const)]
    args = [x1, nw, w_up, w_up, conv_w, conv_w, conv_b, conv_b, w_down, fw]
    scratch = [pltpu.VMEM((tm, d), BF16), pltpu.VMEM((tm, d), F32)]
    if seq_rows is None:
        state = jax.ShapeDtypeStruct((m // tm, 2, dff), F32)
        state_spec = pl.BlockSpec((1, 2, tf), lambda i, j: (i, 0, j))
        scratch += [pltpu.VMEM((tm + 8, tf), F32), pltpu.VMEM((tm + 8, tf), F32),
                    pltpu.VMEM((2, 8, dff), F32)]
    else:
        assert m // tm == 1 and tm % seq_rows == 0
        in_specs += [pl.BlockSpec((2, tm, tf), lambda i, j: (0, 0, j)),
                     pl.BlockSpec((2, tm, tf), lambda i, j: (0, 0, nf + j))]
        args += [prev, prev]
        state = jax.ShapeDtypeStruct((m, dff), F32)
        state_spec = pl.BlockSpec((tm, tf), lambda i, j: (i, j))
    return pl.pallas_call(
        functools.partial(_ffn_kernel, tm=tm, tf=tf, seq_rows=seq_rows),
        grid=(m // tm, nf),
        in_specs=in_specs,
        out_specs=[pl.BlockSpec((tm, d), row), state_spec, state_spec],
        out_shape=[jax.ShapeDtypeStruct((m, d), F32), state, state],
        scratch_shapes=scratch,
        compiler_params=_params("arbitrary", "arbitrary"),
        name="ffn",
    )(*args)


def _alibi_slopes():
    n = N_GROUPS * HPG
    s = 2.0 ** (-8.0 * np.arange(1, n + 1, dtype=np.float32) / n)
    return jnp.asarray(s, F32).reshape(N_GROUPS, HPG)


def _kv_rows(z, g, rows):
    k = z[SLOT_K + g * HPG:SLOT_K + (g + 1) * HPG, rows]
    v = z[SLOT_V + g * HPG:SLOT_V + (g + 1) * HPG, rows]
    return jnp.stack([k, v], axis=0).transpose(2, 0, 1, 3)


def kernel(x_prompt, x_sample, state_hgrn, cache_kv_g0, cache_kv_g1, cache_kv_g2, state_ffn_conv,
           norm_mix_w, w_in, lb_logits, hgrn_norm_w, w_proj_a, w_proj_b, w_out, norm_ffn_w,
           w_up, ffn_conv_w, ffn_conv_b, w_down, norm_final_w):
    assert w_in.shape[0] == 1, "single-layer trunk"
    _, t, d = x_prompt.shape
    n_seq, n_new, _ = x_sample.shape
    dff = w_down.shape[1]
    slopes = _alibi_slopes()
    lb = jnp.cumsum(jax.nn.softmax(lb_logits.astype(F32), axis=0), axis=0)[0].reshape(N_HA, 1, LANES)
    hnw = hgrn_norm_w[0].reshape(N_HA, 1, LANES)
    nmw = norm_mix_w[0].reshape(1, d)
    nfw = norm_ffn_w[0].reshape(1, d)
    fw = norm_final_w.reshape(1, d)
    n_a, n_qkv = 4 * N_HA * LANES, 3 * N_GROUPS * HPG * LANES
    w_in_b = jnp.concatenate([w_in[0, :, :n_a], w_in[0, :, n_a + n_qkv:], w_in[0, :, n_a:n_a + n_qkv]],
                             axis=1).astype(BF16)
    w_a_b = w_proj_a[0].astype(BF16)
    w_b_b = w_proj_b[0].astype(BF16)
    w_o_b = w_out[0].astype(BF16)
    w_up_b = w_up[0].astype(BF16)
    w_dn_b = w_down[0].astype(BF16)
    cw = ffn_conv_w[0]
    cb = ffn_conv_b[0].reshape(1, 2 * dff)
    caches = (cache_kv_g0, cache_kv_g1, cache_kv_g2)

    xp = x_prompt[0]
    z = _inproj(xp, nmw, w_in_b, 1024, 1280)
    o_a, s_p = _hgrn(z, jnp.zeros((1, N_HA, LANES, LANES), F32), lb, hnw,
                     n_seq=1, rows_per_seq=t, chunk=64, valid=64, rows_per_step=1024)
    o_b = _attn_prompt(z, slopes)
    x1 = _mix(o_a, o_b, z, xp, w_a_b, w_b_b, w_o_b, 512)
    y_p, sg, sa = _ffn(x1, nfw, w_up_b, cw, cb, w_dn_b, fw, tm=512, tf=512)
    conv_p = jnp.concatenate([sg[-1], sa[-1]], axis=1)[None, None]
    kv_p = [_kv_rows(z, g, slice(t - min(win, t), t))[None, None] for g, (win, _) in enumerate(PATTERNS)]

    ms = n_seq * n_new
    xs = x_sample.reshape(ms, d)
    z = _inproj(xs, nmw, w_in_b, ms, 1280)
    pad = 8
    z_a_pad = jnp.pad(z[:SLOT_GATE].reshape(SLOT_GATE, n_seq, n_new, LANES),
                      ((0, 0), (0, 0), (0, pad - n_new), (0, 0)))
    o_a, s_s = _hgrn(z_a_pad.reshape(SLOT_GATE, n_seq * pad, LANES), state_hgrn[0], lb, hnw,
                     n_seq=n_seq, rows_per_seq=pad, chunk=pad, valid=n_new, rows_per_step=pad)
    o_a = o_a.reshape(N_HA, n_seq, pad, LANES)[:, :, :n_new].reshape(N_HA, ms, LANES)
    o_b = _attn_sample(z[SLOT_Q:], caches, slopes, n_seq, n_new).reshape(ms, HPG * LANES).astype(BF16)
    x1 = _mix(o_a, o_b, z, xs, w_a_b, w_b_b, w_o_b, ms)
    prev = jnp.repeat(state_ffn_conv[0].transpose(1, 0, 2), n_new, axis=1)
    y_s, ug, ua = _ffn(x1, nfw, w_up_b, cw, cb, w_dn_b, fw, tm=ms, tf=512, seq_rows=n_new, prev=prev)
    u = jnp.concatenate([ug, ua], axis=1).reshape(n_seq, n_new, 2 * dff)
    conv_s = u[:, n_new - 2:][None]
    kv_s = []
    for g in range(N_GROUPS):
        new = _kv_rows(z, g, slice(0, ms)).reshape(n_seq, n_new, 2, HPG, LANES)
        length = caches[g].shape[2]
        kv_s.append(jnp.concatenate([caches[g][0], new], axis=1)[:, -length:][None])

    return (y_p[None], y_s.reshape(n_seq, n_new, d), s_p[None], s_s[None],
            kv_p[0], kv_s[0], kv_p[1], kv_s[1], kv_p[2], kv_s[2], conv_p, conv_s)
```
